```python
import math
import functools
import jax
import jax.numpy as jnp
from jax import lax
import numpy as np

D_MODEL = 4096
BATCH = 2
SEQ = 4096
DEPTH = 2

CTX_LEN = 256
GRID_W = 64
EPS = 1e-6

DN_HEADS = 16
DN_HEAD_DIM = 128
DN_WIDTH = DN_HEADS * DN_HEAD_DIM
DN_CONV = 4
DN_CHUNK = 64

LRU_WIDTH = 2048
LRU_BLOCKS = 16
LRU_BLOCK_DIM = LRU_WIDTH // LRU_BLOCKS
LRU_CONV = 4
LRU_C = 8.0

SC_WIDTH = D_MODEL
SC_CONV = 3

OFF_LRU = 3 * DN_WIDTH
OFF_BETA = OFF_LRU + LRU_WIDTH
OFF_ALPHA = OFF_BETA + 2 * DN_HEADS
AB_STATE = OFF_ALPHA + 2 * DN_HEADS
OFF_LRU_GATE = AB_STATE + DN_WIDTH
AB_IN = OFF_LRU_GATE + LRU_WIDTH
AB_OUT = DN_WIDTH + LRU_WIDTH

kernel_name = 'hybrid_deltanet_rglru_shortconv_dit'

F32 = jnp.float32


def _rms_norm(x, w):
    xf = x.astype(F32)
    y = xf * lax.rsqrt(jnp.mean(xf * xf, axis=-1, keepdims=True) + EPS)
    return (y * w.astype(F32)).astype(x.dtype)


def _l2norm(x):
    return x * lax.rsqrt(jnp.sum(x * x, axis=-1, keepdims=True) + EPS)


def _rev(t, axis, on):
    return jnp.flip(t, axis=axis) if on else t


def _dw_conv(x, w, b=None):
    k = w.shape[0]
    left = k // 2
    y = lax.conv_general_dilated(
        x, w[:, None, :].astype(x.dtype), window_strides=(1,), padding=[(left, k - 1 - left)],
        dimension_numbers=('NWC', 'WIO', 'NWC'), feature_group_count=x.shape[-1])
    if b is not None:
        y = y + b.astype(x.dtype)
    return y


def _to_col_major(t, rows):
    bsz, length, ch = t.shape
    return t.reshape(bsz, rows, GRID_W, ch).swapaxes(1, 2).reshape(bsz, length, ch)


def _to_raster(t, rows):
    bsz, length, ch = t.shape
    return t.reshape(bsz, GRID_W, rows, ch).swapaxes(1, 2).reshape(bsz, length, ch)


def _short_conv_heads(p, conv_w):
    bsz, length, width = p.shape
    y = jax.nn.silu(_dw_conv(p, conv_w)).astype(F32)
    return y.reshape(bsz, length, width // DN_WIDTH, DN_HEADS, DN_HEAD_DIM).transpose(2, 0, 3, 1, 4)


def _decay_gates(p_beta, p_alpha, a_log, dt_bias):
    bsz, length, _ = p_beta.shape
    beta = jax.nn.sigmoid(p_beta.astype(F32)).reshape(bsz, length, 2, DN_HEADS)
    alpha = p_alpha.astype(F32).reshape(bsz, length, 2, DN_HEADS)
    g = -jnp.exp(a_log.astype(F32)) * jax.nn.softplus(alpha + dt_bias.astype(F32))
    return beta.transpose(2, 0, 3, 1), g.transpose(2, 0, 3, 1)


def _delta_chunks(k, v, beta, g):
    bsz, nh, length, dk = k.shape
    n = length // DN_CHUNK
    kc = k.reshape(bsz, nh, n, DN_CHUNK, dk)
    vc = v.reshape(bsz, nh, n, DN_CHUNK, v.shape[-1])
    bc = beta.reshape(bsz, nh, n, DN_CHUNK)
    g_cum = jnp.cumsum(g.reshape(bsz, nh, n, DN_CHUNK), axis=-1)
    idx = jnp.arange(DN_CHUNK)
    diff = g_cum[..., :, None] - g_cum[..., None, :]
    decay = jnp.exp(jnp.where(idx[:, None] >= idx[None, :], diff, -jnp.inf))
    kk = jnp.einsum('bhntd,bhnsd->bhnts', kc, kc)
    lower = jnp.where(idx[:, None] > idx[None, :], bc[..., :, None] * kk * decay, 0.0)
    t_mat = lower + jnp.eye(DN_CHUNK, dtype=lower.dtype)
    solve = functools.partial(lax.linalg.triangular_solve, left_side=True, lower=True, unit_diagonal=True)
    w = solve(t_mat, (bc * jnp.exp(g_cum))[..., None] * kc)
    u = solve(t_mat, bc[..., None] * vc)
    k_end = kc * jnp.exp(g_cum[..., -1:] - g_cum)[..., None]
    g_end = jnp.exp(g_cum[..., -1])
    return g_cum, decay, kc, w, u, k_end, g_end


def _delta_step(s, w_c, u_c, ke_c, ge_c):
    u_c = u_c - jnp.einsum('bhtd,bhdv->bhtv', w_c, s)
    s_new = ge_c[..., None, None] * s + jnp.einsum('bhtd,bhtv->bhdv', ke_c, u_c)
    return u_c, s_new


def _delta_final_state(k, v, beta, g, s0):
    _, _, _, w, u, k_end, g_end = _delta_chunks(k, v, beta, g)

    def step(s, xs):
        _, s = _delta_step(s, *xs)
        return s, None

    s, _ = lax.scan(step, s0, tuple(jnp.moveaxis(t, 2, 0) for t in (w, u, k_end, g_end)))
    return s


def _delta_outputs(q, k, v, beta, g, s0):
    g_cum, decay, kc, w, u, k_end, g_end = _delta_chunks(k, v, beta, g)
    bsz, nh, length, dk = q.shape
    qc = q.reshape(bsz, nh, -1, DN_CHUNK, dk)
    a_qk = jnp.einsum('bhntd,bhnsd->bhnts', qc, kc) * decay
    q_g = qc * jnp.exp(g_cum)[..., None]

    def step(s, xs):
        w_c, u_c, ke_c, ge_c, aqk_c, qg_c = xs
        u_c, s_new = _delta_step(s, w_c, u_c, ke_c, ge_c)
        o = jnp.einsum('bhtd,bhdv->bhtv', qg_c, s) + jnp.einsum('bhts,bhsv->bhtv', aqk_c, u_c)
        return s_new, o

    xs = tuple(jnp.moveaxis(t, 2, 0) for t in (w, u, k_end, g_end, a_qk, q_g))
    _, o = lax.scan(step, s0, xs)
    return jnp.moveaxis(o, 0, 2).reshape(bsz, nh, length, -1)


def _rglru_gates(xc, w_r, b_r, w_i, b_i, lam):
    blocks = xc.reshape(xc.shape[0], xc.shape[1], LRU_BLOCKS, LRU_BLOCK_DIM)
    r = jax.nn.sigmoid(jnp.einsum('blnd,nde->blne', blocks, w_r).reshape(xc.shape) + b_r)
    i = jax.nn.sigmoid(jnp.einsum('blnd,nde->blne', blocks, w_i).reshape(xc.shape) + b_i)
    log_a = -LRU_C * r * jax.nn.softplus(-lam)
    b = jnp.sqrt(-jnp.expm1(2.0 * log_a)) * (i * xc)
    return log_a, b


def _linear_scan(a, b, h0):
    b = b.at[:, 0].add(a[:, 0] * h0)

    def combine(prev, nxt):
        return prev[0] * nxt[0], nxt[0] * prev[1] + nxt[1]

    _, h = lax.associative_scan(combine, (a, b), axis=1)
    return h


def _linear_final_state(log_a, b):
    suffix = lax.cumsum(log_a, axis=1, reverse=True) - log_a
    return jnp.sum(jnp.exp(suffix) * b, axis=1)


def _ab_mixer(h, hc, w_in, qkv_conv, a_log, dt_bias, dn_norm, lru_conv_w, lru_conv_b,
              lru_w_r, lru_b_r, lru_w_i, lru_b_i, lru_lambda, w_out):
    bsz, length, _ = h.shape
    rows = length // GRID_W
    sh = DN_WIDTH
    proj = h @ w_in
    proj_c = hc @ w_in[:, sh:AB_STATE]

    qkv = _short_conv_heads(proj[..., :OFF_LRU], qkv_conv)
    q = _l2norm(qkv[0]) * DN_HEAD_DIM ** -0.5
    k = _l2norm(qkv[1])
    v = qkv[2]
    beta, g = _decay_gates(proj[..., OFF_BETA:OFF_ALPHA], proj[..., OFF_ALPHA:AB_STATE], a_log, dt_bias)
    kv_c = _short_conv_heads(proj_c[..., :OFF_LRU - sh], qkv_conv[:, sh:])
    k_c = _l2norm(kv_c[0])
    v_c = kv_c[1]
    beta_c, g_c = _decay_gates(proj_c[..., OFF_BETA - sh:OFF_ALPHA - sh], proj_c[..., OFF_ALPHA - sh:],
                               a_log, dt_bias)
    s0 = jnp.zeros((bsz, DN_HEADS, DN_HEAD_DIM, DN_HEAD_DIM), F32)
    o_dn = jnp.zeros_like(v)
    for d in range(2):
        s_ctx = _delta_final_state(_rev(k_c, 2, d), _rev(v_c, 2, d), _rev(beta_c[d], 2, d),
                                   _rev(g_c[d], 2, d), s0)
        o = _delta_outputs(_rev(q, 2, d), _rev(k, 2, d), _rev(v, 2, d), _rev(beta[d], 2, d),
                           _rev(g[d], 2, d), s_ctx)
        o_dn = o_dn + _rev(o, 2, d)
    o_dn = _rms_norm(o_dn, dn_norm).transpose(0, 2, 1, 3).reshape(bsz, length, DN_WIDTH)

    xc = _dw_conv(_to_col_major(proj[..., OFF_LRU:OFF_BETA], rows), lru_conv_w, lru_conv_b).astype(F32)
    xc_c = _dw_conv(proj_c[..., OFF_LRU - sh:OFF_BETA - sh], lru_conv_w, lru_conv_b).astype(F32)
    h_lru = jnp.zeros_like(xc)
    for d in range(2):
        la_c, b_c = _rglru_gates(_rev(xc_c, 1, d), lru_w_r[d], lru_b_r[d], lru_w_i[d], lru_b_i[d], lru_lambda[d])
        h0 = _linear_final_state(la_c, b_c)
        la, bb = _rglru_gates(_rev(xc, 1, d), lru_w_r[d], lru_b_r[d], lru_w_i[d], lru_b_i[d], lru_lambda[d])
        h_lru = h_lru + _rev(_linear_scan(jnp.exp(la), bb, h0), 1, d)
    h_lru = _to_raster(h_lru, rows)

    y = jnp.concatenate([o_dn * jax.nn.silu(proj[..., AB_STATE:OFF_LRU_GATE]),
                         h_lru * jax.nn.silu(proj[..., OFF_LRU_GATE:])], axis=-1)
    return y @ w_out


def _sc_mixer(h, w_in, conv_w, w_out):
    bsz, length, _ = h.shape
    rows = length // GRID_W
    b_g, c_g, x_in, gate = jnp.split(h @ w_in, 4, axis=-1)
    z = (c_g * x_in).reshape(bsz * rows, GRID_W, SC_WIDTH)
    z = _dw_conv(z, conv_w).reshape(bsz, length, SC_WIDTH)
    return (b_g * z * jax.nn.silu(gate)) @ w_out


def setup_inputs(seed: int = 0) -> dict:
    key = jax.random.key(seed)
    ks = iter(jax.random.split(key, 32))
    ne, no = (DEPTH + 1) // 2, DEPTH // 2
    dm = D_MODEL

    def nrm(shape, scale):
        return jax.random.normal(next(ks), shape, F32) * scale

    a_pow = jax.random.uniform(next(ks), (ne, 2, LRU_WIDTH), F32, 0.9, 0.999)
    a_base = a_pow ** (1.0 / LRU_C)
    lru_lambda = jnp.log(a_base) - jnp.log1p(-a_base)
    a_mag = jax.random.uniform(next(ks), (ne, 2, DN_HEADS), F32, 1.0, 16.0)
    dt = jnp.exp(jax.random.uniform(next(ks), (ne, 2, DN_HEADS), F32, math.log(1e-3), math.log(1e-1)))
    dt_bias = dt + jnp.log(-jnp.expm1(-dt))
    return {
        'x': nrm((BATCH, SEQ, dm), 1.0),
        'c': nrm((BATCH, dm), 1.0),
        'ctx': nrm((BATCH, CTX_LEN, dm), 1.0),
        'c_ctx': nrm((dm,), 1.0),
        'mod_w': nrm((DEPTH, dm, 3 * dm), dm ** -0.5),
        'mod_b': nrm((DEPTH, 3 * dm), 0.02),
        'norm_w': 1.0 + nrm((DEPTH, dm), 0.02),
        'ab_w_in': nrm((ne, dm, AB_IN), dm ** -0.5),
        'ab_qkv_conv': nrm((ne, DN_CONV, 3 * DN_WIDTH), DN_CONV ** -0.5),
        'ab_a_log': jnp.log(a_mag),
        'ab_dt_bias': dt_bias,
        'ab_dn_norm': 1.0 + nrm((ne, DN_HEAD_DIM), 0.02),
        'ab_lru_conv_w': nrm((ne, LRU_CONV, LRU_WIDTH), LRU_CONV ** -0.5),
        'ab_lru_conv_b': nrm((ne, LRU_WIDTH), 0.02),
        'ab_lru_w_r': nrm((ne, 2, LRU_BLOCKS, LRU_BLOCK_DIM, LRU_BLOCK_DIM), LRU_BLOCK_DIM ** -0.5),
        'ab_lru_b_r': nrm((ne, 2, LRU_WIDTH), 0.02),
        'ab_lru_w_i': nrm((ne, 2, LRU_BLOCKS, LRU_BLOCK_DIM, LRU_BLOCK_DIM), LRU_BLOCK_DIM ** -0.5),
        'ab_lru_b_i': nrm((ne, 2, LRU_WIDTH), 0.02),
        'ab_lru_lambda': lru_lambda,
        'ab_w_out': nrm((ne, AB_OUT, dm), AB_OUT ** -0.5),
        'sc_w_in': nrm((no, dm, 4 * SC_WIDTH), dm ** -0.5),
        'sc_conv': nrm((no, SC_CONV, SC_WIDTH), SC_CONV ** -0.5),
        'sc_w_out': nrm((no, SC_WIDTH, dm), SC_WIDTH ** -0.5),
        'final_norm_w': 1.0 + nrm((dm,), 0.02),
    }


def reference(x, c, ctx, c_ctx, mod_w, mod_b, norm_w, ab_w_in, ab_qkv_conv, ab_a_log, ab_dt_bias,
              ab_dn_norm, ab_lru_conv_w, ab_lru_conv_b, ab_lru_w_r, ab_lru_b_r, ab_lru_w_i, ab_lru_b_i,
              ab_lru_lambda, ab_w_out, sc_w_in, sc_conv, sc_w_out, final_norm_w):
    dm = D_MODEL
    silu_c = jax.nn.silu(c)
    silu_cc = jax.nn.silu(c_ctx)
    for layer in range(DEPTH):
        j = layer // 2
        shift, scale, gate = jnp.split(silu_c @ mod_w[layer] + mod_b[layer], 3, axis=-1)
        hn = _rms_norm(x, norm_w[layer]) * (1.0 + scale[:, None, :]) + shift[:, None, :]
        if layer % 2 == 0:
            shift_c, scale_c = jnp.split(silu_cc @ mod_w[layer][:, :2 * dm] + mod_b[layer][:2 * dm], 2)
            hc = _rms_norm(ctx, norm_w[layer]) * (1.0 + scale_c) + shift_c
            y = _ab_mixer(hn, hc, ab_w_in[j], ab_qkv_conv[j], ab_a_log[j], ab_dt_bias[j], ab_dn_norm[j],
                          ab_lru_conv_w[j], ab_lru_conv_b[j], ab_lru_w_r[j], ab_lru_b_r[j], ab_lru_w_i[j],
                          ab_lru_b_i[j], ab_lru_lambda[j], ab_w_out[j])
        else:
            y = _sc_mixer(hn, sc_w_in[j], sc_conv[j], sc_w_out[j])
        x = x + (gate[:, None, :] * y).astype(x.dtype)
    return _rms_norm(x, final_norm_w)
```

```python
import functools

import jax
import jax.numpy as jnp
from jax import lax
from jax.experimental import pallas as pl
from jax.experimental.pallas import tpu as pltpu

F32 = jnp.float32
BF16 = jnp.bfloat16

D_MODEL = 4096
BATCH = 2
SEQ = 4096
CTX_LEN = 256
GRID_W = 64
GRID_ROWS = SEQ // GRID_W
EPS = 1e-6

DN_HEADS = 16
DN_HEAD_DIM = 128
DN_WIDTH = DN_HEADS * DN_HEAD_DIM
DN_CONV = 4
DN_CHUNK = 64

LRU_WIDTH = 2048
LRU_BLOCKS = 16
LRU_BLOCK_DIM = LRU_WIDTH // LRU_BLOCKS
LRU_CONV = 4
LRU_C = 8.0

SC_WIDTH = D_MODEL
SC_CONV = 3

OFF_LRU = 3 * DN_WIDTH
OFF_BETA = OFF_LRU + LRU_WIDTH
OFF_ALPHA = OFF_BETA + 2 * DN_HEADS
AB_STATE = OFF_ALPHA + 2 * DN_HEADS

P_K = DN_WIDTH
P_V = 2 * DN_WIDTH
P_LRU = 3 * DN_WIDTH
P_DNG = P_LRU + LRU_WIDTH
P_LRG = P_DNG + DN_WIDTH
P_WIDTH = P_LRG + LRU_WIDTH
LANES = 128
GATE_SLOTS = LANES // DN_HEADS

N_CTX_CHUNKS = CTX_LEN // DN_CHUNK
N_SEQ_CHUNKS = SEQ // DN_CHUNK
N_SLOTS = N_CTX_CHUNKS + N_SEQ_CHUNKS
ALL_LEN = CTX_LEN + SEQ

LRU_SEGS = 8
NEG_BIG = -1e30

VMEM_LIMIT = 56 * 1024 * 1024


def _params(*sem):
    return pltpu.CompilerParams(dimension_semantics=sem, vmem_limit_bytes=VMEM_LIMIT)


def _mm(a, b):
    return jnp.dot(a, b, preferred_element_type=F32)


def _mo(v, m):
    return v if isinstance(v, int) else pl.multiple_of(v, m)


def _silu(x):
    return x * jax.nn.sigmoid(x)


def _softplus(x):
    return jnp.maximum(x, 0.0) + jnp.log1p(jnp.exp(-jnp.abs(x)))


def _mod_kernel(c_ref, w_ref, b_ref, o_ref):
    s = _silu(c_ref[...]).astype(BF16)
    o_ref[...] = _mm(s, w_ref[...].astype(BF16)) + b_ref[...]


def _modulation(c8, mod_w, mod_b):
    depth, dm, n = mod_w.shape
    tn = 512
    return pl.pallas_call(
        _mod_kernel,
        out_shape=jax.ShapeDtypeStruct((depth, 8, n), F32),
        grid=(depth, n // tn),
        in_specs=[
            pl.BlockSpec((8, dm), lambda l, j: (0, 0)),
            pl.BlockSpec((None, dm, tn), lambda l, j: (l, 0, j)),
            pl.BlockSpec((None, 1, tn), lambda l, j: (l, 0, j)),
        ],
        out_specs=pl.BlockSpec((None, 8, tn), lambda l, j: (l, 0, j)),
        compiler_params=_params("arbitrary", "arbitrary"),
        name="modulation",
    )(c8, mod_w, mod_b.reshape(depth, 1, n))


def _norm_rows(x_ref, nw_ref, sc_ref, sh_ref, hn_ref, rows_per_step=16):
    tm = x_ref.shape[0]
    nw = nw_ref[...]
    sc = 1.0 + sc_ref[...]
    sh = sh_ref[...]

    def body(r, _):
        rows = pl.ds(pl.multiple_of(r * rows_per_step, rows_per_step), rows_per_step)
        x = x_ref[rows, :]
        ms = jnp.mean(x * x, axis=-1, keepdims=True)
        y = x * lax.rsqrt(ms + EPS) * nw
        hn_ref[rows, :] = (y * sc + sh).astype(BF16)
        return 0

    lax.fori_loop(0, tm // rows_per_step, body, 0)


def _norm_proj_kernel(x_ref, nw_ref, sc_ref, sh_ref, w_ref, wba_ref, o_ref, ba_ref, hn_ref):
    @pl.when(pl.program_id(1) == 0)
    def _():
        _norm_rows(x_ref, nw_ref, sc_ref, sh_ref, hn_ref)
        ba_ref[...] = _mm(hn_ref[...], wba_ref[...])

    o_ref[...] = _mm(hn_ref[...], w_ref[...])


def _norm_proj(x2d, nw, sc3, sh3, mod_row_of_tile, w, wba, col_tile0, n_col_tiles, tm=512, tn=512):
    m, dm = x2d.shape
    return pl.pallas_call(
        _norm_proj_kernel,
        out_shape=(jax.ShapeDtypeStruct((m, n_col_tiles * tn), F32), jax.ShapeDtypeStruct((m, LANES), F32)),
        grid=(m // tm, n_col_tiles),
        in_specs=[
            pl.BlockSpec((tm, dm), lambda i, j: (i, 0)),
            pl.BlockSpec((1, dm), lambda i, j: (0, 0)),
            pl.BlockSpec((None, 1, dm), lambda i, j: (mod_row_of_tile(i), 0, 0)),
            pl.BlockSpec((None, 1, dm), lambda i, j: (mod_row_of_tile(i), 0, 0)),
            pl.BlockSpec((dm, tn), lambda i, j: (0, j + col_tile0)),
            pl.BlockSpec((dm, LANES), lambda i, j: (0, 0)),
        ],
        out_specs=(pl.BlockSpec((tm, tn), lambda i, j: (i, j)), pl.BlockSpec((tm, LANES), lambda i, j: (i, 0))),
        scratch_shapes=[pltpu.VMEM((tm, dm), BF16)],
        compiler_params=_params("arbitrary", "arbitrary"),
        name="norm_proj",
    )(x2d, nw, sc3, sh3, w, wba)


def _split3(a):
    a1 = a.astype(BF16)
    r1 = a - a1.astype(F32)
    a2 = r1.astype(BF16)
    a3 = (r1 - a2.astype(F32)).astype(BF16)
    return a1, a2, a3


def _gates_kernel(ba_ref, alog_ref, dt_ref, o_ref):
    tm = ba_ref.shape[0]
    r = lax.broadcasted_iota(jnp.int32, (DN_CHUNK, DN_CHUNK), 0)
    c = lax.broadcasted_iota(jnp.int32, (DN_CHUNK, DN_CHUNK), 1)
    lower = (r >= c).astype(BF16)
    upper = (r <= c).astype(BF16)
    ones = jnp.ones((DN_CHUNK, DN_CHUNK), BF16)
    slot = lax.broadcasted_iota(jnp.int32, (DN_CHUNK, LANES), 1) % GATE_SLOTS
    neg_a = -jnp.exp(alog_ref[...])
    dt = dt_ref[...]

    def body(i, _):
        rows = pl.ds(pl.multiple_of(i * DN_CHUNK, DN_CHUNK), DN_CHUNK)
        x = ba_ref[rows, :]
        beta = jax.nn.sigmoid(x)
        g = neg_a * _softplus(x + dt)
        g1, g2, g3 = _split3(g)
        cum = _mm(lower, g1) + _mm(lower, g2) + _mm(lower, g3)
        suf = _mm(upper, g1) + _mm(upper, g2) + _mm(upper, g3)
        tot = _mm(ones, g1) + _mm(ones, g2) + _mm(ones, g3)
        o_ref[rows, :] = jnp.where(slot < 2, beta, jnp.where(slot == 2, cum, jnp.where(slot == 3, suf, tot)))
        return 0

    lax.fori_loop(0, tm // DN_CHUNK, body, 0)


def _gates(ba, alog_l, dt_l, tm=512):
    m = ba.shape[0]
    return pl.pallas_call(
        _gates_kernel,
        out_shape=jax.ShapeDtypeStruct((m, LANES), F32),
        grid=(m // tm,),
        in_specs=[
            pl.BlockSpec((tm, LANES), lambda i: (i, 0)),
            pl.BlockSpec((1, LANES), lambda i: (0, 0)),
            pl.BlockSpec((1, LANES), lambda i: (0, 0)),
        ],
        out_specs=pl.BlockSpec((tm, LANES), lambda i: (i, 0)),
        compiler_params=_params("arbitrary"),
        name="dn_gates",
    )(ba, alog_l, dt_l)


HALO = 8


def _fill_halo_scratch(x_ref, xs_ref):
    length, width = x_ref.shape
    zeros = jnp.zeros((HALO, width), F32)
    xs_ref[pl.ds(0, HALO), :] = zeros
    xs_ref[pl.ds(HALO + length, HALO), :] = zeros

    def body(i, _):
        rows = pl.ds(pl.multiple_of(i * DN_CHUNK, DN_CHUNK), DN_CHUNK)
        xs_ref[pl.ds(pl.multiple_of(i * DN_CHUNK + HALO, HALO), DN_CHUNK), :] = x_ref[rows, :]
        return 0

    lax.fori_loop(0, length // DN_CHUNK, body, 0)


def _conv4_chunk(xs_ref, i, w):
    v = xs_ref[pl.ds(pl.multiple_of(i * DN_CHUNK, DN_CHUNK), DN_CHUNK + 2 * HALO), :]
    n = DN_CHUNK
    return (w[0:1] * v[HALO - 2:HALO - 2 + n] + w[1:2] * v[HALO - 1:HALO - 1 + n]
            + w[2:3] * v[HALO:HALO + n] + w[3:4] * v[HALO + 1:HALO + 1 + n])


def _dn_conv_kernel(x_ref, w_ref, o_ref, xs_ref, *, n_q_tiles, n_qk_tiles):
    j = pl.program_id(1)
    length, width = x_ref.shape
    _fill_halo_scratch(x_ref, xs_ref)
    w = w_ref[...]
    q_scale = jnp.where(j < n_q_tiles, DN_HEAD_DIM ** -0.5, 1.0)
    normed = j < n_qk_tiles

    def body(i, _):
        a = _silu(_conv4_chunk(xs_ref, i, w))
        parts = []
        for hh in range(width // DN_HEAD_DIM):
            ah = a[:, hh * DN_HEAD_DIM:(hh + 1) * DN_HEAD_DIM]
            ss = jnp.sum(ah * ah, axis=-1, keepdims=True)
            scale = jnp.where(normed, lax.rsqrt(ss + EPS) * q_scale, 1.0)
            parts.append(ah * scale)
        o_ref[pl.ds(pl.multiple_of(i * DN_CHUNK, DN_CHUNK), DN_CHUNK), :] = jnp.concatenate(parts, axis=1)
        return 0

    lax.fori_loop(0, length // DN_CHUNK, body, 0)


def _dn_conv(p2d, conv_w, length, n_tiles, w_tile0, n_q_tiles, n_qk_tiles, tc=256):
    nb = p2d.shape[0] // length
    return pl.pallas_call(
        functools.partial(_dn_conv_kernel, n_q_tiles=n_q_tiles, n_qk_tiles=n_qk_tiles),
        out_shape=jax.ShapeDtypeStruct((nb * length, n_tiles * tc), F32),
        grid=(nb, n_tiles),
        in_specs=[
            pl.BlockSpec((length, tc), lambda b, j: (b, j)),
            pl.BlockSpec((DN_CONV, tc), lambda b, j: (0, j + w_tile0)),
        ],
        out_specs=pl.BlockSpec((length, tc), lambda b, j: (b, j)),
        scratch_shapes=[pltpu.VMEM((length + 2 * HALO, tc), F32)],
        compiler_params=_params("arbitrary", "arbitrary"),
        name="dn_conv",
    )(p2d, conv_w)


def _split2(a):
    hi = a.astype(BF16)
    lo = (a - hi.astype(F32)).astype(BF16)
    return hi, lo


def _mm3(ah, al, bh, bl):
    return _mm(ah, bh) + _mm(ah, bl) + _mm(al, bh)


def _unit_tri_inverse(a, lower):
    n = a.shape[0]
    r = lax.broadcasted_iota(jnp.int32, (n, n), 0)
    c = lax.broadcasted_iota(jnp.int32, (n, n), 1)
    if not lower:
        r, c = c, r

    def joins(s):
        return (r // (2 * s) == c // (2 * s)) & (r % (2 * s) >= s) & (c % (2 * s) < s)

    d = jnp.where(r == c, 1.0, 0.0) - jnp.where(joins(1), a, 0.0)
    s = 2
    while s < n:
        lh, ll = _split2(jnp.where(joins(s), a, 0.0))
        dh, dl = _split2(d)
        eh, el = _split2(_mm3(dh, dl, lh, ll))
        d = d - _mm3(eh, el, dh, dl)
        s *= 2
    return d


def _dn_chunk_local(h, kc, vc, qc, g, gt_rows, slot, wq_ref, u_ref, ka_ref, ge_ref):
    n = DN_CHUNK
    lane = lax.broadcasted_iota(jnp.int32, (n, LANES), 1)
    r = lax.broadcasted_iota(jnp.int32, (n, n), 0)
    c = lax.broadcasted_iota(jnp.int32, (n, n), 1)

    def col(s):
        return jnp.sum(jnp.where(lane == h * GATE_SLOTS + s, g, 0.0), axis=1, keepdims=True)

    kb = kc.astype(BF16)
    lhs = kb if qc is None else jnp.concatenate([kb, qc.astype(BF16)], axis=0)
    kq = lax.dot_general(lhs, kb, (((1,), (1,)), ((), ())), preferred_element_type=F32)
    kk = kq[:n]
    for d in range(2):
        incl = (r >= c) if d == 0 else (r <= c)
        strict = (r > c) if d == 0 else (r < c)
        b_col = col(d)
        g_col = col(2 + d)
        tot_col = col(4 + d)
        g_row = gt_rows[2 + d:3 + d, :]
        decay = jnp.exp(jnp.where(incl, g_col - g_row, NEG_BIG))
        t_inv = _unit_tri_inverse(jnp.where(strict, b_col * kk * decay, 0.0), d == 0)
        eg = jnp.exp(g_col)
        rhs = jnp.concatenate([kc * (b_col * eg), vc * b_col], axis=1)
        th, tl = _split2(t_inv)
        rh, rl = _split2(rhs)
        wu = _mm3(th, tl, rh, rl)
        w = wu[:, :DN_HEAD_DIM]
        u = wu[:, DN_HEAD_DIM:]
        ke_t = (kc * jnp.exp(tot_col - g_col)).T
        wq_rows = _mo(slot * 2 * n, 2 * n)
        ka_rows = _mo(slot * 3 * n, n)
        wq_ref[d, pl.ds(wq_rows, n), :] = w.astype(BF16)
        u_ref[d, pl.ds(_mo(slot * n, n), n), :] = u
        ka_ref[d, pl.ds(ka_rows, 2 * n), :] = ke_t.astype(BF16)
        ge_ref[d, pl.ds(_mo(slot * 8, 8), 8), :] = jnp.broadcast_to(jnp.exp(tot_col[:8]), (8, LANES))
        if qc is not None:
            wq_ref[d, pl.ds(wq_rows + n, n), :] = (qc * eg).astype(BF16)
            ka_ref[d, pl.ds(ka_rows + 2 * n, n), :] = (kq[n:] * decay).astype(BF16)


def _dn_state_step(d, slot, s, wq_ref, u_ref, ka_ref, ge_ref, with_output):
    n = DN_CHUNK
    wq_rows = _mo(slot * 2 * n, 2 * n)
    ka_rows = _mo(slot * 3 * n, n)
    sb = s.astype(BF16)
    ge = jnp.tile(ge_ref[d, pl.ds(_mo(slot * 8, 8), 8), :], (DN_HEAD_DIM // 8, 1))
    u0 = u_ref[d, pl.ds(_mo(slot * n, n), n), :]
    if with_output:
        rr = _mm(wq_ref[d, pl.ds(wq_rows, 2 * n), :], sb)
        u_c = u0 - rr[:n]
        m = _mm(ka_ref[d, pl.ds(ka_rows, 3 * n), :], u_c.astype(BF16))
        return s * ge + m[:2 * n], rr[n:] + m[2 * n:]
    u_c = u0 - _mm(wq_ref[d, pl.ds(wq_rows, n), :], sb)
    m = _mm(ka_ref[d, pl.ds(ka_rows, 2 * n), :], u_c.astype(BF16))
    return s * ge + m, None


def _deltanet_kernel(q_ref, k_ref, v_ref, kc_ref, vc_ref, g_ref, gt_ref, gate_ref, nw_ref, o_ref,
                     wq_ref, u_ref, ka_ref, ge_ref, acc_ref):
    h = pl.program_id(1)
    n = DN_CHUNK

    for p in range(N_CTX_CHUNKS // 2):
        g2 = g_ref[pl.ds(2 * n * p, 2 * n), :]
        gt2 = gt_ref[:, pl.ds(2 * n * p, 2 * n)]
        for e in range(2):
            rows = pl.ds((2 * p + e) * n, n)
            _dn_chunk_local(h, kc_ref[rows, :], vc_ref[rows, :], None, g2[e * n:(e + 1) * n],
                            gt2[:, e * n:(e + 1) * n], 2 * p + e, wq_ref, u_ref, ka_ref, ge_ref)

    def local_body(p, _):
        base = pl.multiple_of(p * 2 * n, 2 * n)
        gbase = pl.multiple_of(CTX_LEN + p * 2 * n, 2 * n)
        g2 = g_ref[pl.ds(gbase, 2 * n), :]
        gt2 = gt_ref[:, pl.ds(gbase, 2 * n)]
        for e in range(2):
            rows = pl.ds(base + e * n, n)
            _dn_chunk_local(h, k_ref[rows, :], v_ref[rows, :], q_ref[rows, :], g2[e * n:(e + 1) * n],
                            gt2[:, e * n:(e + 1) * n], N_CTX_CHUNKS + 2 * p + e, wq_ref, u_ref, ka_ref, ge_ref)
        return 0

    lax.fori_loop(0, N_SEQ_CHUNKS // 2, local_body, 0)

    s_f = jnp.zeros((DN_HEAD_DIM, DN_HEAD_DIM), F32)
    s_b = jnp.zeros((DN_HEAD_DIM, DN_HEAD_DIM), F32)
    for i in range(N_CTX_CHUNKS):
        s_f, _ = _dn_state_step(0, i, s_f, wq_ref, u_ref, ka_ref, ge_ref, False)
        s_b, _ = _dn_state_step(1, N_CTX_CHUNKS - 1 - i, s_b, wq_ref, u_ref, ka_ref, ge_ref, False)

    acc_ref[...] = jnp.zeros_like(acc_ref)

    def state_body(i, carry):
        s_f, s_b = carry
        s_f, o_f = _dn_state_step(0, N_CTX_CHUNKS + i, s_f, wq_ref, u_ref, ka_ref, ge_ref, True)
        rows_f = pl.ds(pl.multiple_of(i * n, n), n)
        acc_ref[rows_f, :] += o_f
        jb = N_SEQ_CHUNKS - 1 - i
        s_b, o_b = _dn_state_step(1, N_CTX_CHUNKS + jb, s_b, wq_ref, u_ref, ka_ref, ge_ref, True)
        rows_b = pl.ds(pl.multiple_of(jb * n, n), n)
        acc_ref[rows_b, :] += o_b
        return s_f, s_b

    lax.fori_loop(0, N_SEQ_CHUNKS, state_body, (s_f, s_b))

    nw = nw_ref[...]

    def out_body(i, _):
        rows = pl.ds(pl.multiple_of(i * n, n), n)
        o = acc_ref[rows, :]
        ms = jnp.mean(o * o, axis=-1, keepdims=True)
        o_ref[rows, :] = (o * lax.rsqrt(ms + EPS) * nw * _silu(gate_ref[rows, :])).astype(BF16)
        return 0

    lax.fori_loop(0, N_SEQ_CHUNKS, out_body, 0)


def _deltanet(qkvn, kvn_c, gates_tok, gates_t, proj, dn_norm):
    n = DN_CHUNK
    hb = DN_WIDTH // DN_HEAD_DIM
    return pl.pallas_call(
        _deltanet_kernel,
        out_shape=jax.ShapeDtypeStruct((BATCH * SEQ, DN_WIDTH), BF16),
        grid=(BATCH, DN_HEADS),
        in_specs=[
            pl.BlockSpec((SEQ, DN_HEAD_DIM), lambda b, h: (b, h)),
            pl.BlockSpec((SEQ, DN_HEAD_DIM), lambda b, h: (b, hb + h)),
            pl.BlockSpec((SEQ, DN_HEAD_DIM), lambda b, h: (b, 2 * hb + h)),
            pl.BlockSpec((CTX_LEN, DN_HEAD_DIM), lambda b, h: (b, h)),
            pl.BlockSpec((CTX_LEN, DN_HEAD_DIM), lambda b, h: (b, hb + h)),
            pl.BlockSpec((None, ALL_LEN, LANES), lambda b, h: (b, 0, 0)),
            pl.BlockSpec((None, GATE_SLOTS, ALL_LEN), lambda b, h: (b, h, 0)),
            pl.BlockSpec((SEQ, DN_HEAD_DIM), lambda b, h: (b, P_DNG // DN_HEAD_DIM + h)),
            pl.BlockSpec((1, DN_HEAD_DIM), lambda b, h: (0, 0)),
        ],
        out_specs=pl.BlockSpec((SEQ, DN_HEAD_DIM), lambda b, h: (b, h)),
        scratch_shapes=[
            pltpu.VMEM((2, N_SLOTS * 2 * n, DN_HEAD_DIM), BF16),
            pltpu.VMEM((2, N_SLOTS * n, DN_HEAD_DIM), F32),
            pltpu.VMEM((2, N_SLOTS * 3 * n, n), BF16),
            pltpu.VMEM((2, N_SLOTS * 8, LANES), F32),
            pltpu.VMEM((SEQ, DN_HEAD_DIM), F32),
        ],
        compiler_params=_params("arbitrary", "arbitrary"),
        name="deltanet",
    )(qkvn, qkvn, qkvn, kvn_c, kvn_c, gates_tok, gates_t, proj, dn_norm)


def _shift_down(y):
    row = lax.broadcasted_iota(jnp.int32, y.shape, 0)
    return jnp.where(row == 0, 0.0, pltpu.roll(y, 1, axis=0))


def _shift_up(y):
    row = lax.broadcasted_iota(jnp.int32, y.shape, 0)
    return jnp.where(row == y.shape[0] - 1, 0.0, pltpu.roll(y, y.shape[0] - 1, axis=0))


def _lru_gate_rows(xc, wcat, bcat, sp, a_ref, b_ref, rows):
    g = _mm(xc.astype(BF16), wcat) + bcat
    w = LRU_BLOCK_DIM
    for d in range(2):
        r = jax.nn.sigmoid(g[:, (2 * d) * w:(2 * d + 1) * w])
        i = jax.nn.sigmoid(g[:, (2 * d + 1) * w:(2 * d + 2) * w])
        log_a = -LRU_C * r * sp[d:d + 1, :]
        a_ref[d, rows, :] = jnp.exp(log_a)
        b_ref[d, rows, :] = jnp.sqrt(1.0 - jnp.exp(2.0 * log_a)) * (i * xc)


def _seg_rows(j, steps, stride_rows, colmajor):
    if colmajor:
        start = (j % GRID_ROWS) * GRID_W + j // GRID_ROWS
        return pl.ds(start, LRU_SEGS, stride=GRID_W // LRU_SEGS)
    return pl.ds(j, LRU_SEGS, stride=steps)


def _carry_chain(h_end, p_end, c0, reverse):
    row = lax.broadcasted_iota(jnp.int32, h_end.shape, 0)
    carries = jnp.zeros_like(h_end)
    c = c0
    order = range(LRU_SEGS - 1, -1, -1) if reverse else range(LRU_SEGS)
    for gseg in order:
        carries = jnp.where(row == gseg, c, carries)
        c = h_end[gseg:gseg + 1, :] + p_end[gseg:gseg + 1, :] * c
    return carries, c


def _lru_kernel(x_ref, xc_in_ref, gate_ref, cw_ref, cb_ref, wcat_ref, bcat_ref, lam_ref, o_ref,
                xc_ref, xs_ref, xcc_ref, a_ref, b_ref, ac_ref, bc_ref, hl_ref, pp_ref, hs_ref):
    cw = cw_ref[...]
    cb = cb_ref[...]
    wcat = wcat_ref[...]
    bcat = bcat_ref[...]
    sp = _softplus(-lam_ref[...])
    gw = GRID_W

    _fill_halo_scratch(xc_in_ref, xs_ref)
    for i in range(CTX_LEN // DN_CHUNK):
        rows = pl.ds(i * DN_CHUNK, DN_CHUNK)
        xcc_ref[rows, :] = _conv4_chunk(xs_ref, i, cw) + cb
    _lru_gate_rows(xcc_ref[...], wcat, bcat, sp, ac_ref, bc_ref, pl.ds(0, CTX_LEN))

    csteps = CTX_LEN // LRU_SEGS
    zero = jnp.zeros((LRU_SEGS, LRU_BLOCK_DIM), F32)
    one = jnp.ones((LRU_SEGS, LRU_BLOCK_DIM), F32)

    def ctx_body(j, carry):
        hf, pf, hb, pb = carry
        rf = _seg_rows(j, csteps, None, False)
        a = ac_ref[0, rf, :]
        hf = a * hf + bc_ref[0, rf, :]
        pf = a * pf
        rb = _seg_rows(csteps - 1 - j, csteps, None, False)
        a = ac_ref[1, rb, :]
        hb = a * hb + bc_ref[1, rb, :]
        pb = a * pb
        return hf, pf, hb, pb

    hf, pf, hb, pb = lax.fori_loop(0, csteps, ctx_body, (zero, one, zero, one))
    zrow = jnp.zeros((1, LRU_BLOCK_DIM), F32)
    _, h0_f = _carry_chain(hf, pf, zrow, False)
    _, h0_b = _carry_chain(hb, pb, zrow, True)

    def slab(r):
        return x_ref[pl.ds(pl.multiple_of(r * gw, gw), gw), :]

    def conv_slab(r, xm2, xm1, x0, xp1):
        xc_ref[pl.ds(pl.multiple_of(r * gw, gw), gw), :] = (
            cb + cw[0:1] * xm2 + cw[1:2] * xm1 + cw[2:3] * x0 + cw[3:4] * xp1)

    last = GRID_ROWS - 1
    conv_slab(0, _shift_down(slab(last - 1)), _shift_down(slab(last)), slab(0), slab(1))
    conv_slab(1, _shift_down(slab(last)), slab(0), slab(1), slab(2))
    conv_slab(last, slab(last - 2), slab(last - 1), slab(last), _shift_up(slab(0)))

    def conv_body(r, _):
        conv_slab(r, slab(r - 2), slab(r - 1), slab(r), slab(r + 1))
        return 0

    lax.fori_loop(2, last, conv_body, 0)

    gate_rows = 256

    def gate_body(i, _):
        rows = pl.ds(pl.multiple_of(i * gate_rows, gate_rows), gate_rows)
        _lru_gate_rows(xc_ref[rows, :], wcat, bcat, sp, a_ref, b_ref, rows)
        return 0

    lax.fori_loop(0, SEQ // gate_rows, gate_body, 0)

    steps = SEQ // LRU_SEGS

    def scan_body(j, carry):
        hf, pf, hb, pb = carry
        rf = _seg_rows(j, steps, None, True)
        a = a_ref[0, rf, :]
        hf = a * hf + b_ref[0, rf, :]
        pf = a * pf
        hl_ref[0, j] = hf
        pp_ref[0, j] = pf
        jb = steps - 1 - j
        rb = _seg_rows(jb, steps, None, True)
        a = a_ref[1, rb, :]
        hb = a * hb + b_ref[1, rb, :]
        pb = a * pb
        hl_ref[1, jb] = hb
        pp_ref[1, jb] = pb
        return hf, pf, hb, pb

    hf, pf, hb, pb = lax.fori_loop(0, steps, scan_body, (zero, one, zero, one))
    c_f, _ = _carry_chain(hf, pf, h0_f, False)
    c_b, _ = _carry_chain(hb, pb, h0_b, True)

    def fix_body(j, _):
        h = hl_ref[0, j] + pp_ref[0, j] * c_f + hl_ref[1, j] + pp_ref[1, j] * c_b
        hs_ref[_seg_rows(j, steps, None, True), :] = h
        return 0

    lax.fori_loop(0, steps, fix_body, 0)

    def out_body(i, _):
        rows = pl.ds(pl.multiple_of(i * gate_rows, gate_rows), gate_rows)
        o_ref[rows, :] = (hs_ref[rows, :] * _silu(gate_ref[rows, :])).astype(BF16)
        return 0

    lax.fori_loop(0, SEQ // gate_rows, out_body, 0)


def _lru(proj, projc, conv_w, conv_b, wcat, bcat, lam):
    w = LRU_BLOCK_DIM
    steps = SEQ // LRU_SEGS
    return pl.pallas_call(
        _lru_kernel,
        out_shape=jax.ShapeDtypeStruct((BATCH * SEQ, LRU_WIDTH), BF16),
        grid=(BATCH, LRU_BLOCKS),
        in_specs=[
            pl.BlockSpec((SEQ, w), lambda b, n: (b, P_LRU // w + n)),
            pl.BlockSpec((CTX_LEN, w), lambda b, n: (b, (P_LRU - DN_WIDTH) // w + n)),
            pl.BlockSpec((SEQ, w), lambda b, n: (b, P_LRG // w + n)),
            pl.BlockSpec((LRU_CONV, w), lambda b, n: (0, n)),
            pl.BlockSpec((1, w), lambda b, n: (0, n)),
            pl.BlockSpec((None, w, 4 * w), lambda b, n: (n, 0, 0)),
            pl.BlockSpec((None, 1, 4 * w), lambda b, n: (n, 0, 0)),
            pl.BlockSpec((None, 2, w), lambda b, n: (n, 0, 0)),
        ],
        out_specs=pl.BlockSpec((SEQ, w), lambda b, n: (b, n)),
        scratch_shapes=[
            pltpu.VMEM((SEQ, w), F32),
            pltpu.VMEM((CTX_LEN + 2 * HALO, w), F32),
            pltpu.VMEM((CTX_LEN, w), F32),
            pltpu.VMEM((2, SEQ, w), F32),
            pltpu.VMEM((2, SEQ, w), F32),
            pltpu.VMEM((2, CTX_LEN, w), F32),
            pltpu.VMEM((2, CTX_LEN, w), F32),
            pltpu.VMEM((2, steps, LRU_SEGS, w), F32),
            pltpu.VMEM((2, steps, LRU_SEGS, w), F32),
            pltpu.VMEM((SEQ, w), F32),
        ],
        compiler_params=_params("arbitrary", "arbitrary"),
        name="rglru",
    )(proj, projc, proj, conv_w, conv_b, wcat, bcat, lam)


def _out_proj_kernel(ya_ref, yb_ref, wa_ref, wb_ref, x_ref, g_ref, o_ref):
    acc = _mm(ya_ref[...], wa_ref[...]) + _mm(yb_ref[...], wb_ref[...])
    o_ref[...] = x_ref[...] + g_ref[...] * acc


def _out_proj(ya, yb, ya_blk, yb_blk, w, x2d, gate3, tm=512, tn=1024):
    m, dm = x2d.shape
    kh = w.shape[0] // 2
    per_batch = SEQ // tm
    return pl.pallas_call(
        _out_proj_kernel,
        out_shape=jax.ShapeDtypeStruct((m, dm), F32),
        grid=(m // tm, dm // tn),
        in_specs=[
            pl.BlockSpec((tm, kh), lambda i, j: (i, ya_blk)),
            pl.BlockSpec((tm, kh), lambda i, j: (i, yb_blk)),
            pl.BlockSpec((kh, tn), lambda i, j: (0, j)),
            pl.BlockSpec((kh, tn), lambda i, j: (1, j)),
            pl.BlockSpec((tm, tn), lambda i, j: (i, j)),
            pl.BlockSpec((None, 1, tn), lambda i, j: (i // per_batch, 0, j)),
        ],
        out_specs=pl.BlockSpec((tm, tn), lambda i, j: (i, j)),
        compiler_params=_params("arbitrary", "arbitrary"),
        name="out_proj",
    )(ya, yb, w, w, x2d, gate3)


def _sc_proj_kernel(x_ref, nw_ref, sc_ref, sh_ref, wb_ref, wc_ref, wx_ref, wg_ref, cw_ref, o_ref,
                    hn_ref, z_ref, bg_ref):
    @pl.when(pl.program_id(1) == 0)
    def _():
        _norm_rows(x_ref, nw_ref, sc_ref, sh_ref, hn_ref)

    hn = hn_ref[...]
    z_ref[...] = _mm(hn, wc_ref[...]) * _mm(hn, wx_ref[...])
    bg_ref[...] = _mm(hn, wb_ref[...]) * _silu(_mm(hn, wg_ref[...]))
    cw = cw_ref[...]
    gw = GRID_W

    def body(r, _):
        rows = pl.ds(pl.multiple_of(r * gw, gw), gw)
        z = z_ref[rows, :]
        zc = cw[0:1] * _shift_down(z) + cw[1:2] * z + cw[2:3] * _shift_up(z)
        o_ref[rows, :] = (bg_ref[rows, :] * zc).astype(BF16)
        return 0

    lax.fori_loop(0, x_ref.shape[0] // gw, body, 0)


def _sc_proj(x2d, nw, sc3, sh3, w, conv_w, tm=512, tc=256):
    m, dm = x2d.shape
    nt = SC_WIDTH // tc
    per_batch = SEQ // tm
    mod_spec = pl.BlockSpec((None, 1, dm), lambda i, j: (i // per_batch, 0, 0))
    return pl.pallas_call(
        _sc_proj_kernel,
        out_shape=jax.ShapeDtypeStruct((m, SC_WIDTH), BF16),
        grid=(m // tm, nt),
        in_specs=[
            pl.BlockSpec((tm, dm), lambda i, j: (i, 0)),
            pl.BlockSpec((1, dm), lambda i, j: (0, 0)),
            mod_spec,
            mod_spec,
            pl.BlockSpec((dm, tc), lambda i, j: (0, j)),
            pl.BlockSpec((dm, tc), lambda i, j: (0, nt + j)),
            pl.BlockSpec((dm, tc), lambda i, j: (0, 2 * nt + j)),
            pl.BlockSpec((dm, tc), lambda i, j: (0, 3 * nt + j)),
            pl.BlockSpec((SC_CONV, tc), lambda i, j: (0, j)),
        ],
        out_specs=pl.BlockSpec((tm, tc), lambda i, j: (i, j)),
        scratch_shapes=[pltpu.VMEM((tm, dm), BF16), pltpu.VMEM((tm, tc), F32), pltpu.VMEM((tm, tc), F32)],
        compiler_params=_params("arbitrary", "arbitrary"),
        name="sc_proj",
    )(x2d, nw, sc3, sh3, w, w, w, w, conv_w)


def _final_norm_kernel(x_ref, w_ref, o_ref):
    w = w_ref[...]

    def body(r, _):
        rows = pl.ds(pl.multiple_of(r * 16, 16), 16)
        x = x_ref[rows, :]
        ms = jnp.mean(x * x, axis=-1, keepdims=True)
        o_ref[rows, :] = x * lax.rsqrt(ms + EPS) * w
        return 0

    lax.fori_loop(0, x_ref.shape[0] // 16, body, 0)


def _final_norm(x2d, w, tm=256):
    m, dm = x2d.shape
    return pl.pallas_call(
        _final_norm_kernel,
        out_shape=jax.ShapeDtypeStruct((m, dm), F32),
        grid=(m // tm,),
        in_specs=[pl.BlockSpec((tm, dm), lambda i: (i, 0)), pl.BlockSpec((1, dm), lambda i: (0, 0))],
        out_specs=pl.BlockSpec((tm, dm), lambda i: (i, 0)),
        compiler_params=_params("arbitrary"),
        name="final_norm",
    )(x2d, w)


def _gate_lane_vector(p):
    z = jnp.zeros((DN_HEADS,), F32)
    cols = jnp.stack([z, z, p[0], p[1], p[0], p[1], z, z], axis=1)
    return cols.reshape(1, LANES).astype(F32)


def _gate_weight(w_in):
    h = jnp.arange(DN_HEADS)
    idx = jnp.stack([OFF_BETA + h, OFF_BETA + DN_HEADS + h, OFF_ALPHA + h, OFF_ALPHA + DN_HEADS + h,
                     OFF_ALPHA + h, OFF_ALPHA + DN_HEADS + h, OFF_BETA + h, OFF_BETA + h], axis=1).reshape(-1)
    live = (jnp.arange(LANES) % GATE_SLOTS) < 6
    return jnp.where(live[None, :], w_in[:, idx], 0.0).astype(BF16)


def kernel(x, c, ctx, c_ctx, mod_w, mod_b, norm_w, ab_w_in, ab_qkv_conv, ab_a_log, ab_dt_bias, ab_dn_norm,
           ab_lru_conv_w, ab_lru_conv_b, ab_lru_w_r, ab_lru_b_r, ab_lru_w_i, ab_lru_b_i, ab_lru_lambda, ab_w_out,
           sc_w_in, sc_conv, sc_w_out, final_norm_w):
    dm = D_MODEL
    x2d = x.reshape(BATCH * SEQ, dm)
    ctx2d = ctx.reshape(BATCH * CTX_LEN, dm)

    c8 = jnp.concatenate([c, c_ctx[None, :], jnp.zeros((8 - BATCH - 1, dm), F32)], axis=0)
    mod = _modulation(c8, mod_w, mod_b)
    shift = [mod[l, :, :dm].reshape(8, 1, dm) for l in range(2)]
    scale = [mod[l, :, dm:2 * dm].reshape(8, 1, dm) for l in range(2)]
    gate = [mod[l, :, 2 * dm:].reshape(8, 1, dm) for l in range(2)]

    w_in = ab_w_in[0]
    w_main = jnp.concatenate([w_in[:, :OFF_BETA], w_in[:, AB_STATE:]], axis=1).astype(BF16)
    w_ba = _gate_weight(w_in)
    nw0 = norm_w[0].reshape(1, dm)
    tm, tn = 512, 512
    proj, ba = _norm_proj(x2d, nw0, scale[0], shift[0], lambda i: i // (SEQ // tm), w_main, w_ba,
                          0, P_WIDTH // tn, tm, tn)
    projc, ba_c = _norm_proj(ctx2d, nw0, scale[0], shift[0], lambda i: BATCH, w_main, w_ba,
                             P_K // tn, (P_DNG - P_K) // tn, tm, tn)

    ba_all = jnp.concatenate([ba_c.reshape(BATCH, CTX_LEN, LANES), ba.reshape(BATCH, SEQ, LANES)], axis=1)
    gates_tok = _gates(ba_all.reshape(BATCH * ALL_LEN, LANES), _gate_lane_vector(ab_a_log[0]),
                       _gate_lane_vector(ab_dt_bias[0])).reshape(BATCH, ALL_LEN, LANES)
    gates_t = gates_tok.transpose(0, 2, 1)

    qkv_conv = ab_qkv_conv[0]
    tc = 256
    qkvn = _dn_conv(proj, qkv_conv, SEQ, 3 * DN_WIDTH // tc, 0, DN_WIDTH // tc, 2 * DN_WIDTH // tc, tc)
    kvn_c = _dn_conv(projc, qkv_conv, CTX_LEN, 2 * DN_WIDTH // tc, DN_WIDTH // tc, 0, DN_WIDTH // tc, tc)
    y_dn = _deltanet(qkvn, kvn_c, gates_tok, gates_t, proj, ab_dn_norm[0].reshape(1, DN_HEAD_DIM))

    wcat = jnp.concatenate([ab_lru_w_r[0, 0], ab_lru_w_i[0, 0], ab_lru_w_r[0, 1], ab_lru_w_i[0, 1]],
                           axis=-1).astype(BF16)
    bl = LRU_BLOCK_DIM
    bcat = jnp.concatenate([ab_lru_b_r[0, 0].reshape(LRU_BLOCKS, 1, bl), ab_lru_b_i[0, 0].reshape(LRU_BLOCKS, 1, bl),
                            ab_lru_b_r[0, 1].reshape(LRU_BLOCKS, 1, bl), ab_lru_b_i[0, 1].reshape(LRU_BLOCKS, 1, bl)],
                           axis=-1)
    lam = ab_lru_lambda[0].reshape(2, LRU_BLOCKS, bl).transpose(1, 0, 2)
    y_lru = _lru(proj, projc, ab_lru_conv_w[0], ab_lru_conv_b[0].reshape(1, LRU_WIDTH), wcat, bcat, lam)

    x1 = _out_proj(y_dn, y_lru, 0, 0, ab_w_out[0].astype(BF16), x2d, gate[0])

    y1 = _sc_proj(x1, norm_w[1].reshape(1, dm), scale[1], shift[1], sc_w_in[0].astype(BF16), sc_conv[0])
    x2 = _out_proj(y1, y1, 0, 1, sc_w_out[0].astype(BF16), x1, gate[1])

    return _final_norm(x2, final_norm_w.reshape(1, dm)).reshape(BATCH, SEQ, dm)
```

```python
import functools

import jax
import jax.numpy as jnp
import numpy as np
from jax import lax
from jax.experimental import pallas as pl
from jax.experimental.pallas import tpu as pltpu

F32 = jnp.float32
BF16 = jnp.bfloat16

D_MODEL = 4096
BATCH = 2
SEQ = 4096
CTX_LEN = 256
GRID_W = 64
GRID_ROWS = SEQ // GRID_W
EPS = 1e-6

DN_HEADS = 16
DN_HEAD_DIM = 128
DN_WIDTH = DN_HEADS * DN_HEAD_DIM
DN_CONV = 4
DN_CHUNK = 64

LRU_WIDTH = 2048
LRU_BLOCKS = 16
LRU_BLOCK_DIM = LRU_WIDTH // LRU_BLOCKS
LRU_CONV = 4
LRU_C = 8.0

SC_WIDTH = D_MODEL
SC_CONV = 3

OFF_LRU = 3 * DN_WIDTH
OFF_BETA = OFF_LRU + LRU_WIDTH
OFF_ALPHA = OFF_BETA + 2 * DN_HEADS
AB_STATE = OFF_ALPHA + 2 * DN_HEADS

P_K = DN_WIDTH
P_V = 2 * DN_WIDTH
P_LRU = 3 * DN_WIDTH
P_DNG = P_LRU + LRU_WIDTH
P_LRG = P_DNG + DN_WIDTH
P_WIDTH = P_LRG + LRU_WIDTH
LANES = 128
GATE_SLOTS = LANES // DN_HEADS

N_CTX_CHUNKS = CTX_LEN // DN_CHUNK
N_SEQ_CHUNKS = SEQ // DN_CHUNK
N_SLOTS = N_CTX_CHUNKS + N_SEQ_CHUNKS
ALL_LEN = CTX_LEN + SEQ

LRU_SEGS = 8
NEG_BIG = -1e30

VMEM_LIMIT = 56 * 1024 * 1024


def _params(*sem):
    return pltpu.CompilerParams(dimension_semantics=sem, vmem_limit_bytes=VMEM_LIMIT)


def _mm(a, b):
    return jnp.dot(a, b, preferred_element_type=F32)


def _mo(v, m):
    return v if isinstance(v, int) else pl.multiple_of(v, m)


def _silu(x):
    return x * jax.nn.sigmoid(x)


def _softplus(x):
    return jnp.maximum(x, 0.0) + jnp.log1p(jnp.exp(-jnp.abs(x)))


def _mod_kernel(c_ref, w_ref, b_ref, o_ref):
    s = _silu(c_ref[...]).astype(BF16)
    o_ref[...] = _mm(s, w_ref[...].astype(BF16)) + b_ref[...]


def _modulation(c8, mod_w, mod_b):
    depth, dm, n = mod_w.shape
    tn = 512
    return pl.pallas_call(
        _mod_kernel,
        out_shape=jax.ShapeDtypeStruct((depth, 8, n), F32),
        grid=(depth, n // tn),
        in_specs=[
            pl.BlockSpec((8, dm), lambda l, j: (0, 0)),
            pl.BlockSpec((None, dm, tn), lambda l, j: (l, 0, j)),
            pl.BlockSpec((None, 1, tn), lambda l, j: (l, 0, j)),
        ],
        out_specs=pl.BlockSpec((None, 8, tn), lambda l, j: (l, 0, j)),
        compiler_params=_params("arbitrary", "arbitrary"),
        name="modulation",
    )(c8, mod_w, mod_b.reshape(depth, 1, n))


def _norm_rows(x_ref, nw_ref, sc_ref, sh_ref, hn_ref, rows_per_step=16):
    tm = x_ref.shape[0]
    nw = nw_ref[...]
    sc = 1.0 + sc_ref[...]
    sh = sh_ref[...]

    def body(r, _):
        rows = pl.ds(pl.multiple_of(r * rows_per_step, rows_per_step), rows_per_step)
        x = x_ref[rows, :]
        ms = jnp.mean(x * x, axis=-1, keepdims=True)
        y = x * lax.rsqrt(ms + EPS) * nw
        hn_ref[rows, :] = (y * sc + sh).astype(BF16)
        return 0

    lax.fori_loop(0, tm // rows_per_step, body, 0)


def _norm_proj_kernel(x_ref, nw_ref, sc_ref, sh_ref, w_ref, wba_ref, o_ref, ba_ref, hn_ref):
    @pl.when(pl.program_id(1) == 0)
    def _():
        _norm_rows(x_ref, nw_ref, sc_ref, sh_ref, hn_ref)
        ba_ref[...] = _mm(hn_ref[...], wba_ref[...])

    o_ref[...] = _mm(hn_ref[...], w_ref[...])


def _norm_proj(x2d, nw, sc3, sh3, mod_row_of_tile, w, wba, col_tile0, n_col_tiles, tm=512, tn=512):
    m, dm = x2d.shape
    return pl.pallas_call(
        _norm_proj_kernel,
        out_shape=(jax.ShapeDtypeStruct((m, n_col_tiles * tn), F32), jax.ShapeDtypeStruct((m, LANES), F32)),
        grid=(m // tm, n_col_tiles),
        in_specs=[
            pl.BlockSpec((tm, dm), lambda i, j: (i, 0)),
            pl.BlockSpec((1, dm), lambda i, j: (0, 0)),
            pl.BlockSpec((None, 1, dm), lambda i, j: (mod_row_of_tile(i), 0, 0)),
            pl.BlockSpec((None, 1, dm), lambda i, j: (mod_row_of_tile(i), 0, 0)),
            pl.BlockSpec((dm, tn), lambda i, j: (0, j + col_tile0)),
            pl.BlockSpec((dm, LANES), lambda i, j: (0, 0)),
        ],
        out_specs=(pl.BlockSpec((tm, tn), lambda i, j: (i, j)), pl.BlockSpec((tm, LANES), lambda i, j: (i, 0))),
        scratch_shapes=[pltpu.VMEM((tm, dm), BF16)],
        compiler_params=_params("arbitrary", "arbitrary"),
        name="norm_proj",
    )(x2d, nw, sc3, sh3, w, wba)


def _split3(a):
    a1 = a.astype(BF16)
    r1 = a - a1.astype(F32)
    a2 = r1.astype(BF16)
    a3 = (r1 - a2.astype(F32)).astype(BF16)
    return a1, a2, a3


def _gates_kernel(ba_ref, alog_ref, dt_ref, o_ref):
    tm = ba_ref.shape[0]
    r = lax.broadcasted_iota(jnp.int32, (DN_CHUNK, DN_CHUNK), 0)
    c = lax.broadcasted_iota(jnp.int32, (DN_CHUNK, DN_CHUNK), 1)
    lower = (r >= c).astype(BF16)
    upper = (r <= c).astype(BF16)
    ones = jnp.ones((DN_CHUNK, DN_CHUNK), BF16)
    slot = lax.broadcasted_iota(jnp.int32, (DN_CHUNK, LANES), 1) % GATE_SLOTS
    neg_a = -jnp.exp(alog_ref[...])
    dt = dt_ref[...]

    def body(i, _):
        rows = pl.ds(pl.multiple_of(i * DN_CHUNK, DN_CHUNK), DN_CHUNK)
        x = ba_ref[rows, :]
        beta = jax.nn.sigmoid(x)
        g = neg_a * _softplus(x + dt)
        g1, g2, g3 = _split3(g)
        cum = _mm(lower, g1) + _mm(lower, g2) + _mm(lower, g3)
        suf = _mm(upper, g1) + _mm(upper, g2) + _mm(upper, g3)
        tot = _mm(ones, g1) + _mm(ones, g2) + _mm(ones, g3)
        o_ref[rows, :] = jnp.where(slot < 2, beta, jnp.where(slot == 2, cum, jnp.where(slot == 3, suf, tot)))
        return 0

    lax.fori_loop(0, tm // DN_CHUNK, body, 0)


def _gates(ba, alog_l, dt_l, tm=512):
    m = ba.shape[0]
    return pl.pallas_call(
        _gates_kernel,
        out_shape=jax.ShapeDtypeStruct((m, LANES), F32),
        grid=(m // tm,),
        in_specs=[
            pl.BlockSpec((tm, LANES), lambda i: (i, 0)),
            pl.BlockSpec((1, LANES), lambda i: (0, 0)),
            pl.BlockSpec((1, LANES), lambda i: (0, 0)),
        ],
        out_specs=pl.BlockSpec((tm, LANES), lambda i: (i, 0)),
        compiler_params=_params("arbitrary"),
        name="dn_gates",
    )(ba, alog_l, dt_l)


HALO = 8


def _fill_halo_scratch(x_ref, xs_ref):
    length, width = x_ref.shape
    zeros = jnp.zeros((HALO, width), F32)
    xs_ref[pl.ds(0, HALO), :] = zeros
    xs_ref[pl.ds(HALO + length, HALO), :] = zeros

    def body(i, _):
        rows = pl.ds(pl.multiple_of(i * DN_CHUNK, DN_CHUNK), DN_CHUNK)
        xs_ref[pl.ds(pl.multiple_of(i * DN_CHUNK + HALO, HALO), DN_CHUNK), :] = x_ref[rows, :]
        return 0

    lax.fori_loop(0, length // DN_CHUNK, body, 0)


def _conv4_chunk(xs_ref, i, w):
    v = xs_ref[pl.ds(pl.multiple_of(i * DN_CHUNK, DN_CHUNK), DN_CHUNK + 2 * HALO), :]
    n = DN_CHUNK
    return (w[0:1] * v[HALO - 2:HALO - 2 + n] + w[1:2] * v[HALO - 1:HALO - 1 + n]
            + w[2:3] * v[HALO:HALO + n] + w[3:4] * v[HALO + 1:HALO + 1 + n])


def _dn_conv_kernel(x_ref, w_ref, o_ref, xs_ref, *, n_q_tiles, n_qk_tiles):
    j = pl.program_id(1)
    length, width = x_ref.shape
    _fill_halo_scratch(x_ref, xs_ref)
    w = w_ref[...]
    q_scale = jnp.where(j < n_q_tiles, DN_HEAD_DIM ** -0.5, 1.0)
    normed = j < n_qk_tiles

    def body(i, _):
        a = _silu(_conv4_chunk(xs_ref, i, w))
        parts = []
        for hh in range(width // DN_HEAD_DIM):
            ah = a[:, hh * DN_HEAD_DIM:(hh + 1) * DN_HEAD_DIM]
            ss = jnp.sum(ah * ah, axis=-1, keepdims=True)
            scale = jnp.where(normed, lax.rsqrt(ss + EPS) * q_scale, 1.0)
            parts.append(ah * scale)
        o_ref[pl.ds(pl.multiple_of(i * DN_CHUNK, DN_CHUNK), DN_CHUNK), :] = jnp.concatenate(parts, axis=1)
        return 0

    lax.fori_loop(0, length // DN_CHUNK, body, 0)


def _dn_conv(p2d, conv_w, length, n_tiles, w_tile0, n_q_tiles, n_qk_tiles, tc=256):
    nb = p2d.shape[0] // length
    return pl.pallas_call(
        functools.partial(_dn_conv_kernel, n_q_tiles=n_q_tiles, n_qk_tiles=n_qk_tiles),
        out_shape=jax.ShapeDtypeStruct((nb * length, n_tiles * tc), F32),
        grid=(nb, n_tiles),
        in_specs=[
            pl.BlockSpec((length, tc), lambda b, j: (b, j)),
            pl.BlockSpec((DN_CONV, tc), lambda b, j: (0, j + w_tile0)),
        ],
        out_specs=pl.BlockSpec((length, tc), lambda b, j: (b, j)),
        scratch_shapes=[pltpu.VMEM((length + 2 * HALO, tc), F32)],
        compiler_params=_params("arbitrary", "arbitrary"),
        name="dn_conv",
    )(p2d, conv_w)


def _split2(a):
    hi = a.astype(BF16)
    lo = (a - hi.astype(F32)).astype(BF16)
    return hi, lo


def _mm3(ah, al, bh, bl):
    return _mm(ah, bh) + _mm(ah, bl) + _mm(al, bh)


DN_GROUP = 8
N_TRI_LEVELS = 6
M_STRICT, M_INCL, M_EYE, M_JOIN = 0, 1, 2, 3


def _tri_masks():
    n = DN_CHUNK
    r = np.arange(n)[:, None].repeat(2 * n, axis=1)
    c = np.arange(2 * n)[None, :].repeat(n, axis=0) % n
    upper = np.arange(2 * n)[None, :] >= n
    rr = np.where(upper, c, r)
    cc = np.where(upper, r, c)
    out = [rr > cc, rr >= cc, rr == cc]
    for lvl in range(N_TRI_LEVELS):
        s = 1 << lvl
        out.append((rr // (2 * s) == cc // (2 * s)) & (rr % (2 * s) >= s) & (cc % (2 * s) < s))
    return jnp.asarray(np.stack(out).astype(np.float32))


def _block_diag2(x):
    lane = lax.broadcasted_iota(jnp.int32, x.shape, 1)
    zero = jnp.zeros_like(x)
    return jnp.concatenate([jnp.where(lane < DN_CHUNK, x, zero), jnp.where(lane >= DN_CHUNK, x, zero)], axis=0)


def _packed_tri_inverse(a_list, m_ref):
    d_list = [m_ref[M_EYE] - m_ref[M_JOIN] * a for a in a_list]
    for lvl in range(1, N_TRI_LEVELS):
        join = m_ref[M_JOIN + lvl]
        l_bd = [tuple(_block_diag2(p) for p in _split2(join * a)) for a in a_list]
        d_parts = [_split2(d) for d in d_list]
        e_parts = [_split2(_mm3(dh, dl, lh, ll)) for (dh, dl), (lh, ll) in zip(d_parts, l_bd)]
        d_bd = [(_block_diag2(dh), _block_diag2(dl)) for dh, dl in d_parts]
        d_list = [d - _mm3(eh, el, bh, bl) for d, (eh, el), (bh, bl) in zip(d_list, e_parts, d_bd)]
    return d_list


def _dn_local_group(h, srcs, g_all, gt_all, slot0, m_ref, nq_ref, c_ref, ge_ref, acc_ref):
    n = DN_CHUNK
    lane = lax.broadcasted_iota(jnp.int32, (n, LANES), 1)
    lane1 = lax.broadcasted_iota(jnp.int32, (1, LANES), 1)
    upper = lane >= n
    strict = m_ref[M_STRICT]
    incl = m_ref[M_INCL] > 0.5
    nt = (((1,), (1,)), ((), ()))
    cols, a_list, decays, kqs = [], [], [], []
    for e, (kc, vc, qc) in enumerate(srcs):
        g = g_all[e * n:(e + 1) * n]

        def col(s, g=g):
            return jnp.sum(jnp.where(lane == h * GATE_SLOTS + s, g, 0.0), axis=1, keepdims=True)

        cs = [col(s) for s in range(6)]
        t0 = (e // 2) * LANES
        row_f = gt_all[2:3, t0:t0 + LANES]
        row_b = gt_all[3:4, t0:t0 + LANES]
        if e % 2 == 0:
            g_row = jnp.where(lane1 < n, row_f, pltpu.roll(row_b, n, axis=1))
        else:
            g_row = jnp.where(lane1 < n, pltpu.roll(row_f, n, axis=1), row_b)
        b_pack = jnp.where(upper, cs[1], cs[0])
        g_pack = jnp.where(upper, cs[3], cs[2])
        decay = jnp.exp(jnp.where(incl, g_pack - g_row, NEG_BIG))
        kb = kc.astype(BF16)
        lhs = kb if qc is None else jnp.concatenate([kb, qc.astype(BF16)], axis=0)
        kq = lax.dot_general(lhs, jnp.concatenate([kb, kb], axis=0), nt, preferred_element_type=F32)
        cols.append(cs)
        decays.append(decay)
        kqs.append(kq)
        a_list.append(strict * (b_pack * kq[:n] * decay))
    t_inv = _packed_tri_inverse(a_list, m_ref)
    for e, (kc, vc, qc) in enumerate(srcs):
        b_f, b_b, g_f, g_b, tot_f, tot_b = cols[e]
        eg = (jnp.exp(g_f), jnp.exp(g_b))
        rhs = jnp.concatenate([jnp.concatenate([kc * (b_f * eg[0]), vc * b_f], axis=1),
                               jnp.concatenate([kc * (b_b * eg[1]), vc * b_b], axis=1)], axis=0)
        th, tl = (_block_diag2(p) for p in _split2(t_inv[e]))
        rh, rl = _split2(rhs)
        wu = _mm3(th, tl, rh, rl)
        slot = slot0 + e
        nq_rows = _mo(slot * 3 * n, n)
        for d in range(2):
            g_col, tot_col = (g_f, tot_f) if d == 0 else (g_b, tot_b)
            ke_t = (kc * jnp.exp(tot_col - g_col)).T.astype(BF16)
            wub = wu[d * n:(d + 1) * n].astype(BF16)
            ge_ref[d, pl.ds(_mo(slot * 8, 8), 8), :] = jnp.broadcast_to(jnp.exp(tot_col[:8]), (8, LANES))
            if qc is None:
                nc = _mm(ke_t, wub)
            else:
                a_qk = (kqs[e][n:] * decays[e])[:, d * n:(d + 1) * n].astype(BF16)
                nc = _mm(jnp.concatenate([ke_t, a_qk], axis=0), wub)
                nq_ref[d, pl.ds(nq_rows + 2 * n, n), :] = (qc * eg[d] - nc[2 * n:, :DN_HEAD_DIM]).astype(BF16)
                rows = pl.ds(_mo((slot - N_CTX_CHUNKS) * n, n), n)
                acc_ref[rows, :] += nc[2 * n:, DN_HEAD_DIM:]
            nq_ref[d, pl.ds(nq_rows, 2 * n), :] = nc[:2 * n, :DN_HEAD_DIM].astype(BF16)
            c_ref[d, pl.ds(_mo(slot * 2 * n, 2 * n), 2 * n), :] = nc[:2 * n, DN_HEAD_DIM:]


def _dn_state_step(d, slot, s, nq_ref, c_ref, ge_ref, with_output):
    n = DN_CHUNK
    nq_rows = _mo(slot * 3 * n, n)
    ge = jnp.tile(ge_ref[d, pl.ds(_mo(slot * 8, 8), 8), :], (DN_HEAD_DIM // 8, 1))
    c = c_ref[d, pl.ds(_mo(slot * 2 * n, 2 * n), 2 * n), :]
    r = _mm(nq_ref[d, pl.ds(nq_rows, (3 if with_output else 2) * n), :], s.astype(BF16))
    return s * ge - r[:2 * n] + c, (r[2 * n:] if with_output else None)


def _deltanet_kernel(q_ref, k_ref, v_ref, kc_ref, vc_ref, g_ref, gt_ref, gate_ref, nw_ref, m_ref, o_ref,
                     nq_ref, c_ref, ge_ref, acc_ref):
    h = pl.program_id(1)
    n = DN_CHUNK
    gl = DN_GROUP * n
    acc_ref[...] = jnp.zeros_like(acc_ref)

    srcs = [(kc_ref[pl.ds(e * n, n), :], vc_ref[pl.ds(e * n, n), :], None) for e in range(N_CTX_CHUNKS)]
    _dn_local_group(h, srcs, g_ref[pl.ds(0, CTX_LEN), :], gt_ref[:, pl.ds(0, CTX_LEN)], 0,
                    m_ref, nq_ref, c_ref, ge_ref, acc_ref)

    def local_body(p, _):
        base = pl.multiple_of(p * gl, gl)
        gbase = pl.multiple_of(CTX_LEN + p * gl, gl)
        srcs = [(k_ref[pl.ds(base + e * n, n), :], v_ref[pl.ds(base + e * n, n), :], q_ref[pl.ds(base + e * n, n), :])
                for e in range(DN_GROUP)]
        _dn_local_group(h, srcs, g_ref[pl.ds(gbase, gl), :], gt_ref[:, pl.ds(gbase, gl)], N_CTX_CHUNKS + p * DN_GROUP,
                        m_ref, nq_ref, c_ref, ge_ref, acc_ref)
        return 0

    lax.fori_loop(0, N_SEQ_CHUNKS // DN_GROUP, local_body, 0)

    s_f = jnp.zeros((DN_HEAD_DIM, DN_HEAD_DIM), F32)
    s_b = jnp.zeros((DN_HEAD_DIM, DN_HEAD_DIM), F32)
    for i in range(N_CTX_CHUNKS):
        s_f, _ = _dn_state_step(0, i, s_f, nq_ref, c_ref, ge_ref, False)
        s_b, _ = _dn_state_step(1, N_CTX_CHUNKS - 1 - i, s_b, nq_ref, c_ref, ge_ref, False)

    def state_body(i, carry):
        s_f, s_b = carry
        s_f, o_f = _dn_state_step(0, N_CTX_CHUNKS + i, s_f, nq_ref, c_ref, ge_ref, True)
        rows_f = pl.ds(pl.multiple_of(i * n, n), n)
        acc_ref[rows_f, :] += o_f
        jb = N_SEQ_CHUNKS - 1 - i
        s_b, o_b = _dn_state_step(1, N_CTX_CHUNKS + jb, s_b, nq_ref, c_ref, ge_ref, True)
        rows_b = pl.ds(pl.multiple_of(jb * n, n), n)
        acc_ref[rows_b, :] += o_b
        return s_f, s_b

    lax.fori_loop(0, N_SEQ_CHUNKS, state_body, (s_f, s_b))

    nw = nw_ref[...]

    def out_body(i, _):
        rows = pl.ds(pl.multiple_of(i * n, n), n)
        o = acc_ref[rows, :]
        ms = jnp.mean(o * o, axis=-1, keepdims=True)
        o_ref[rows, :] = (o * lax.rsqrt(ms + EPS) * nw * _silu(gate_ref[rows, :])).astype(BF16)
        return 0

    lax.fori_loop(0, N_SEQ_CHUNKS, out_body, 0)


def _deltanet(qkvn, kvn_c, gates_tok, gates_t, proj, dn_norm):
    n = DN_CHUNK
    hb = DN_WIDTH // DN_HEAD_DIM
    return pl.pallas_call(
        _deltanet_kernel,
        out_shape=jax.ShapeDtypeStruct((BATCH * SEQ, DN_WIDTH), BF16),
        grid=(BATCH, DN_HEADS),
        in_specs=[
            pl.BlockSpec((SEQ, DN_HEAD_DIM), lambda b, h: (b, h)),
            pl.BlockSpec((SEQ, DN_HEAD_DIM), lambda b, h: (b, hb + h)),
            pl.BlockSpec((SEQ, DN_HEAD_DIM), lambda b, h: (b, 2 * hb + h)),
            pl.BlockSpec((CTX_LEN, DN_HEAD_DIM), lambda b, h: (b, h)),
            pl.BlockSpec((CTX_LEN, DN_HEAD_DIM), lambda b, h: (b, hb + h)),
            pl.BlockSpec((None, ALL_LEN, LANES), lambda b, h: (b, 0, 0)),
            pl.BlockSpec((None, GATE_SLOTS, ALL_LEN), lambda b, h: (b, h, 0)),
            pl.BlockSpec((SEQ, DN_HEAD_DIM), lambda b, h: (b, P_DNG // DN_HEAD_DIM + h)),
            pl.BlockSpec((1, DN_HEAD_DIM), lambda b, h: (0, 0)),
            pl.BlockSpec((M_JOIN + N_TRI_LEVELS, n, 2 * n), lambda b, h: (0, 0, 0)),
        ],
        out_specs=pl.BlockSpec((SEQ, DN_HEAD_DIM), lambda b, h: (b, h)),
        scratch_shapes=[
            pltpu.VMEM((2, N_SLOTS * 3 * n, DN_HEAD_DIM), BF16),
            pltpu.VMEM((2, N_SLOTS * 2 * n, DN_HEAD_DIM), F32),
            pltpu.VMEM((2, N_SLOTS * 8, LANES), F32),
            pltpu.VMEM((SEQ, DN_HEAD_DIM), F32),
        ],
        compiler_params=_params("arbitrary", "arbitrary"),
        name="deltanet",
    )(qkvn, qkvn, qkvn, kvn_c, kvn_c, gates_tok, gates_t, proj, dn_norm, _tri_masks())


def _shift_down(y):
    row = lax.broadcasted_iota(jnp.int32, y.shape, 0)
    return jnp.where(row == 0, 0.0, pltpu.roll(y, 1, axis=0))


def _shift_up(y):
    row = lax.broadcasted_iota(jnp.int32, y.shape, 0)
    return jnp.where(row == y.shape[0] - 1, 0.0, pltpu.roll(y, y.shape[0] - 1, axis=0))


def _lru_gate_rows(xc, wcat, bcat, sp, a_ref, b_ref, rows):
    g = _mm(xc.astype(BF16), wcat) + bcat
    w = LRU_BLOCK_DIM
    for d in range(2):
        r = jax.nn.sigmoid(g[:, (2 * d) * w:(2 * d + 1) * w])
        i = jax.nn.sigmoid(g[:, (2 * d + 1) * w:(2 * d + 2) * w])
        log_a = -LRU_C * r * sp[d:d + 1, :]
        a_ref[d, rows, :] = jnp.exp(log_a)
        b_ref[d, rows, :] = jnp.sqrt(1.0 - jnp.exp(2.0 * log_a)) * (i * xc)


def _seg_rows(j, steps, stride_rows, colmajor):
    if colmajor:
        start = (j % GRID_ROWS) * GRID_W + j // GRID_ROWS
        return pl.ds(start, LRU_SEGS, stride=GRID_W // LRU_SEGS)
    return pl.ds(j, LRU_SEGS, stride=steps)


def _carry_chain(h_end, p_end, c0, reverse):
    row = lax.broadcasted_iota(jnp.int32, h_end.shape, 0)
    carries = jnp.zeros_like(h_end)
    c = c0
    order = range(LRU_SEGS - 1, -1, -1) if reverse else range(LRU_SEGS)
    for gseg in order:
        carries = jnp.where(row == gseg, c, carries)
        c = h_end[gseg:gseg + 1, :] + p_end[gseg:gseg + 1, :] * c
    return carries, c


def _lru_kernel(x_ref, xc_in_ref, gate_ref, cw_ref, cb_ref, wcat_ref, bcat_ref, lam_ref, o_ref,
                xc_ref, xs_ref, xcc_ref, a_ref, b_ref, ac_ref, bc_ref, hl_ref, pp_ref, hs_ref):
    cw = cw_ref[...]
    cb = cb_ref[...]
    wcat = wcat_ref[...]
    bcat = bcat_ref[...]
    sp = _softplus(-lam_ref[...])
    gw = GRID_W

    _fill_halo_scratch(xc_in_ref, xs_ref)
    for i in range(CTX_LEN // DN_CHUNK):
        rows = pl.ds(i * DN_CHUNK, DN_CHUNK)
        xcc_ref[rows, :] = _conv4_chunk(xs_ref, i, cw) + cb
    _lru_gate_rows(xcc_ref[...], wcat, bcat, sp, ac_ref, bc_ref, pl.ds(0, CTX_LEN))

    csteps = CTX_LEN // LRU_SEGS
    zero = jnp.zeros((LRU_SEGS, LRU_BLOCK_DIM), F32)
    one = jnp.ones((LRU_SEGS, LRU_BLOCK_DIM), F32)

    def ctx_body(j, carry):
        hf, pf, hb, pb = carry
        rf = _seg_rows(j, csteps, None, False)
        a = ac_ref[0, rf, :]
        hf = a * hf + bc_ref[0, rf, :]
        pf = a * pf
        rb = _seg_rows(csteps - 1 - j, csteps, None, False)
        a = ac_ref[1, rb, :]
        hb = a * hb + bc_ref[1, rb, :]
        pb = a * pb
        return hf, pf, hb, pb

    hf, pf, hb, pb = lax.fori_loop(0, csteps, ctx_body, (zero, one, zero, one))
    zrow = jnp.zeros((1, LRU_BLOCK_DIM), F32)
    _, h0_f = _carry_chain(hf, pf, zrow, False)
    _, h0_b = _carry_chain(hb, pb, zrow, True)

    def slab(r):
        return x_ref[pl.ds(pl.multiple_of(r * gw, gw), gw), :]

    def conv_slab(r, xm2, xm1, x0, xp1):
        xc_ref[pl.ds(pl.multiple_of(r * gw, gw), gw), :] = (
            cb + cw[0:1] * xm2 + cw[1:2] * xm1 + cw[2:3] * x0 + cw[3:4] * xp1)

    last = GRID_ROWS - 1
    conv_slab(0, _shift_down(slab(last - 1)), _shift_down(slab(last)), slab(0), slab(1))
    conv_slab(1, _shift_down(slab(last)), slab(0), slab(1), slab(2))
    conv_slab(last, slab(last - 2), slab(last - 1), slab(last), _shift_up(slab(0)))

    def conv_body(r, _):
        conv_slab(r, slab(r - 2), slab(r - 1), slab(r), slab(r + 1))
        return 0

    lax.fori_loop(2, last, conv_body, 0)

    gate_rows = 256

    def gate_body(i, _):
        rows = pl.ds(pl.multiple_of(i * gate_rows, gate_rows), gate_rows)
        _lru_gate_rows(xc_ref[rows, :], wcat, bcat, sp, a_ref, b_ref, rows)
        return 0

    lax.fori_loop(0, SEQ // gate_rows, gate_body, 0)

    steps = SEQ // LRU_SEGS

    def scan_body(j, carry):
        hf, pf, hb, pb = carry
        rf = _seg_rows(j, steps, None, True)
        a = a_ref[0, rf, :]
        hf = a * hf + b_ref[0, rf, :]
        pf = a * pf
        hl_ref[0, j] = hf
        pp_ref[0, j] = pf
        jb = steps - 1 - j
        rb = _seg_rows(jb, steps, None, True)
        a = a_ref[1, rb, :]
        hb = a * hb + b_ref[1, rb, :]
        pb = a * pb
        hl_ref[1, jb] = hb
        pp_ref[1, jb] = pb
        return hf, pf, hb, pb

    hf, pf, hb, pb = lax.fori_loop(0, steps, scan_body, (zero, one, zero, one))
    c_f, _ = _carry_chain(hf, pf, h0_f, False)
    c_b, _ = _carry_chain(hb, pb, h0_b, True)

    def fix_body(j, _):
        h = hl_ref[0, j] + pp_ref[0, j] * c_f + hl_ref[1, j] + pp_ref[1, j] * c_b
        hs_ref[_seg_rows(j, steps, None, True), :] = h
        return 0

    lax.fori_loop(0, steps, fix_body, 0)

    def out_body(i, _):
        rows = pl.ds(pl.multiple_of(i * gate_rows, gate_rows), gate_rows)
        o_ref[rows, :] = (hs_ref[rows, :] * _silu(gate_ref[rows, :])).astype(BF16)
        return 0

    lax.fori_loop(0, SEQ // gate_rows, out_body, 0)


def _lru(proj, projc, conv_w, conv_b, wcat, bcat, lam):
    w = LRU_BLOCK_DIM
    steps = SEQ // LRU_SEGS
    return pl.pallas_call(
        _lru_kernel,
        out_shape=jax.ShapeDtypeStruct((BATCH * SEQ, LRU_WIDTH), BF16),
        grid=(BATCH, LRU_BLOCKS),
        in_specs=[
            pl.BlockSpec((SEQ, w), lambda b, n: (b, P_LRU // w + n)),
            pl.BlockSpec((CTX_LEN, w), lambda b, n: (b, (P_LRU - DN_WIDTH) // w + n)),
            pl.BlockSpec((SEQ, w), lambda b, n: (b, P_LRG // w + n)),
            pl.BlockSpec((LRU_CONV, w), lambda b, n: (0, n)),
            pl.BlockSpec((1, w), lambda b, n: (0, n)),
            pl.BlockSpec((None, w, 4 * w), lambda b, n: (n, 0, 0)),
            pl.BlockSpec((None, 1, 4 * w), lambda b, n: (n, 0, 0)),
            pl.BlockSpec((None, 2, w), lambda b, n: (n, 0, 0)),
        ],
        out_specs=pl.BlockSpec((SEQ, w), lambda b, n: (b, n)),
        scratch_shapes=[
            pltpu.VMEM((SEQ, w), F32),
            pltpu.VMEM((CTX_LEN + 2 * HALO, w), F32),
            pltpu.VMEM((CTX_LEN, w), F32),
            pltpu.VMEM((2, SEQ, w), F32),
            pltpu.VMEM((2, SEQ, w), F32),
            pltpu.VMEM((2, CTX_LEN, w), F32),
            pltpu.VMEM((2, CTX_LEN, w), F32),
            pltpu.VMEM((2, steps, LRU_SEGS, w), F32),
            pltpu.VMEM((2, steps, LRU_SEGS, w), F32),
            pltpu.VMEM((SEQ, w), F32),
        ],
        compiler_params=_params("arbitrary", "arbitrary"),
        name="rglru",
    )(proj, projc, proj, conv_w, conv_b, wcat, bcat, lam)


def _out_proj_kernel(ya_ref, yb_ref, wa_ref, wb_ref, x_ref, g_ref, o_ref):
    acc = _mm(ya_ref[...], wa_ref[...]) + _mm(yb_ref[...], wb_ref[...])
    o_ref[...] = x_ref[...] + g_ref[...] * acc


def _out_proj(ya, yb, ya_blk, yb_blk, w, x2d, gate3, tm=512, tn=1024):
    m, dm = x2d.shape
    kh = w.shape[0] // 2
    per_batch = SEQ // tm
    return pl.pallas_call(
        _out_proj_kernel,
        out_shape=jax.ShapeDtypeStruct((m, dm), F32),
        grid=(m // tm, dm // tn),
        in_specs=[
            pl.BlockSpec((tm, kh), lambda i, j: (i, ya_blk)),
            pl.BlockSpec((tm, kh), lambda i, j: (i, yb_blk)),
            pl.BlockSpec((kh, tn), lambda i, j: (0, j)),
            pl.BlockSpec((kh, tn), lambda i, j: (1, j)),
            pl.BlockSpec((tm, tn), lambda i, j: (i, j)),
            pl.BlockSpec((None, 1, tn), lambda i, j: (i // per_batch, 0, j)),
        ],
        out_specs=pl.BlockSpec((tm, tn), lambda i, j: (i, j)),
        compiler_params=_params("arbitrary", "arbitrary"),
        name="out_proj",
    )(ya, yb, w, w, x2d, gate3)


def _sc_proj_kernel(x_ref, nw_ref, sc_ref, sh_ref, wb_ref, wc_ref, wx_ref, wg_ref, cw_ref, o_ref,
                    hn_ref, z_ref, bg_ref):
    @pl.when(pl.program_id(1) == 0)
    def _():
        _norm_rows(x_ref, nw_ref, sc_ref, sh_ref, hn_ref)

    hn = hn_ref[...]
    z_ref[...] = _mm(hn, wc_ref[...]) * _mm(hn, wx_ref[...])
    bg_ref[...] = _mm(hn, wb_ref[...]) * _silu(_mm(hn, wg_ref[...]))
    cw = cw_ref[...]
    gw = GRID_W

    def body(r, _):
        rows = pl.ds(pl.multiple_of(r * gw, gw), gw)
        z = z_ref[rows, :]
        zc = cw[0:1] * _shift_down(z) + cw[1:2] * z + cw[2:3] * _shift_up(z)
        o_ref[rows, :] = (bg_ref[rows, :] * zc).astype(BF16)
        return 0

    lax.fori_loop(0, x_ref.shape[0] // gw, body, 0)


def _sc_proj(x2d, nw, sc3, sh3, w, conv_w, tm=512, tc=256):
    m, dm = x2d.shape
    nt = SC_WIDTH // tc
    per_batch = SEQ // tm
    mod_spec = pl.BlockSpec((None, 1, dm), lambda i, j: (i // per_batch, 0, 0))
    return pl.pallas_call(
        _sc_proj_kernel,
        out_shape=jax.ShapeDtypeStruct((m, SC_WIDTH), BF16),
        grid=(m // tm, nt),
        in_specs=[
            pl.BlockSpec((tm, dm), lambda i, j: (i, 0)),
            pl.BlockSpec((1, dm), lambda i, j: (0, 0)),
            mod_spec,
            mod_spec,
            pl.BlockSpec((dm, tc), lambda i, j: (0, j)),
            pl.BlockSpec((dm, tc), lambda i, j: (0, nt + j)),
            pl.BlockSpec((dm, tc), lambda i, j: (0, 2 * nt + j)),
            pl.BlockSpec((dm, tc), lambda i, j: (0, 3 * nt + j)),
            pl.BlockSpec((SC_CONV, tc), lambda i, j: (0, j)),
        ],
        out_specs=pl.BlockSpec((tm, tc), lambda i, j: (i, j)),
        scratch_shapes=[pltpu.VMEM((tm, dm), BF16), pltpu.VMEM((tm, tc), F32), pltpu.VMEM((tm, tc), F32)],
        compiler_params=_params("arbitrary", "arbitrary"),
        name="sc_proj",
    )(x2d, nw, sc3, sh3, w, w, w, w, conv_w)


def _final_norm_kernel(x_ref, w_ref, o_ref):
    w = w_ref[...]

    def body(r, _):
        rows = pl.ds(pl.multiple_of(r * 16, 16), 16)
        x = x_ref[rows, :]
        ms = jnp.mean(x * x, axis=-1, keepdims=True)
        o_ref[rows, :] = x * lax.rsqrt(ms + EPS) * w
        return 0

    lax.fori_loop(0, x_ref.shape[0] // 16, body, 0)


def _final_norm(x2d, w, tm=256):
    m, dm = x2d.shape
    return pl.pallas_call(
        _final_norm_kernel,
        out_shape=jax.ShapeDtypeStruct((m, dm), F32),
        grid=(m // tm,),
        in_specs=[pl.BlockSpec((tm, dm), lambda i: (i, 0)), pl.BlockSpec((1, dm), lambda i: (0, 0))],
        out_specs=pl.BlockSpec((tm, dm), lambda i: (i, 0)),
        compiler_params=_params("arbitrary"),
        name="final_norm",
    )(x2d, w)


def _gate_lane_vector(p):
    z = jnp.zeros((DN_HEADS,), F32)
    cols = jnp.stack([z, z, p[0], p[1], p[0], p[1], z, z], axis=1)
    return cols.reshape(1, LANES).astype(F32)


def _gate_weight(w_in):
    h = jnp.arange(DN_HEADS)
    idx = jnp.stack([OFF_BETA + h, OFF_BETA + DN_HEADS + h, OFF_ALPHA + h, OFF_ALPHA + DN_HEADS + h,
                     OFF_ALPHA + h, OFF_ALPHA + DN_HEADS + h, OFF_BETA + h, OFF_BETA + h], axis=1).reshape(-1)
    live = (jnp.arange(LANES) % GATE_SLOTS) < 6
    return jnp.where(live[None, :], w_in[:, idx], 0.0).astype(BF16)


def kernel(x, c, ctx, c_ctx, mod_w, mod_b, norm_w, ab_w_in, ab_qkv_conv, ab_a_log, ab_dt_bias, ab_dn_norm,
           ab_lru_conv_w, ab_lru_conv_b, ab_lru_w_r, ab_lru_b_r, ab_lru_w_i, ab_lru_b_i, ab_lru_lambda, ab_w_out,
           sc_w_in, sc_conv, sc_w_out, final_norm_w):
    dm = D_MODEL
    x2d = x.reshape(BATCH * SEQ, dm)
    ctx2d = ctx.reshape(BATCH * CTX_LEN, dm)

    c8 = jnp.concatenate([c, c_ctx[None, :], jnp.zeros((8 - BATCH - 1, dm), F32)], axis=0)
    mod = _modulation(c8, mod_w, mod_b)
    shift = [mod[l, :, :dm].reshape(8, 1, dm) for l in range(2)]
    scale = [mod[l, :, dm:2 * dm].reshape(8, 1, dm) for l in range(2)]
    gate = [mod[l, :, 2 * dm:].reshape(8, 1, dm) for l in range(2)]

    w_in = ab_w_in[0]
    w_main = jnp.concatenate([w_in[:, :OFF_BETA], w_in[:, AB_STATE:]], axis=1).astype(BF16)
    w_ba = _gate_weight(w_in)
    nw0 = norm_w[0].reshape(1, dm)
    tm, tn = 512, 512
    proj, ba = _norm_proj(x2d, nw0, scale[0], shift[0], lambda i: i // (SEQ // tm), w_main, w_ba,
                          0, P_WIDTH // tn, tm, tn)
    projc, ba_c = _norm_proj(ctx2d, nw0, scale[0], shift[0], lambda i: BATCH, w_main, w_ba,
                             P_K // tn, (P_DNG - P_K) // tn, tm, tn)

    ba_all = jnp.concatenate([ba_c.reshape(BATCH, CTX_LEN, LANES), ba.reshape(BATCH, SEQ, LANES)], axis=1)
    gates_tok = _gates(ba_all.reshape(BATCH * ALL_LEN, LANES), _gate_lane_vector(ab_a_log[0]),
                       _gate_lane_vector(ab_dt_bias[0])).reshape(BATCH, ALL_LEN, LANES)
    gates_t = gates_tok.transpose(0, 2, 1)

    qkv_conv = ab_qkv_conv[0]
    tc = 256
    qkvn = _dn_conv(proj, qkv_conv, SEQ, 3 * DN_WIDTH // tc, 0, DN_WIDTH // tc, 2 * DN_WIDTH // tc, tc)
    kvn_c = _dn_conv(projc, qkv_conv, CTX_LEN, 2 * DN_WIDTH // tc, DN_WIDTH // tc, 0, DN_WIDTH // tc, tc)
    y_dn = _deltanet(qkvn, kvn_c, gates_tok, gates_t, proj, ab_dn_norm[0].reshape(1, DN_HEAD_DIM))

    wcat = jnp.concatenate([ab_lru_w_r[0, 0], ab_lru_w_i[0, 0], ab_lru_w_r[0, 1], ab_lru_w_i[0, 1]],
                           axis=-1).astype(BF16)
    bl = LRU_BLOCK_DIM
    bcat = jnp.concatenate([ab_lru_b_r[0, 0].reshape(LRU_BLOCKS, 1, bl), ab_lru_b_i[0, 0].reshape(LRU_BLOCKS, 1, bl),
                            ab_lru_b_r[0, 1].reshape(LRU_BLOCKS, 1, bl), ab_lru_b_i[0, 1].reshape(LRU_BLOCKS, 1, bl)],
                           axis=-1)
    lam = ab_lru_lambda[0].reshape(2, LRU_BLOCKS, bl).transpose(1, 0, 2)
    y_lru = _lru(proj, projc, ab_lru_conv_w[0], ab_lru_conv_b[0].reshape(1, LRU_WIDTH), wcat, bcat, lam)

    x1 = _out_proj(y_dn, y_lru, 0, 0, ab_w_out[0].astype(BF16), x2d, gate[0])

    y1 = _sc_proj(x1, norm_w[1].reshape(1, dm), scale[1], shift[1], sc_w_in[0].astype(BF16), sc_conv[0])
    x2 = _out_proj(y1, y1, 0, 1, sc_w_out[0].astype(BF16), x1, gate[1])

    return _final_norm(x2, final_norm_w.reshape(1, dm)).reshape(BATCH, SEQ, dm)
```

```python
import functools

import jax
import jax.numpy as jnp
import numpy as np
from jax import lax
from jax.experimental import pallas as pl
from jax.experimental.pallas import tpu as pltpu

F32 = jnp.float32
BF16 = jnp.bfloat16

D_MODEL = 4096
BATCH = 2
SEQ = 4096
CTX_LEN = 256
GRID_W = 64
GRID_ROWS = SEQ // GRID_W
EPS = 1e-6

DN_HEADS = 16
DN_HEAD_DIM = 128
DN_WIDTH = DN_HEADS * DN_HEAD_DIM
DN_CONV = 4
DN_CHUNK = 64

LRU_WIDTH = 2048
LRU_BLOCKS = 16
LRU_BLOCK_DIM = LRU_WIDTH // LRU_BLOCKS
LRU_CONV = 4
LRU_C = 8.0

SC_WIDTH = D_MODEL
SC_CONV = 3

OFF_LRU = 3 * DN_WIDTH
OFF_BETA = OFF_LRU + LRU_WIDTH
OFF_ALPHA = OFF_BETA + 2 * DN_HEADS
AB_STATE = OFF_ALPHA + 2 * DN_HEADS

P_K = DN_WIDTH
P_V = 2 * DN_WIDTH
P_LRU = 3 * DN_WIDTH
P_DNG = P_LRU + LRU_WIDTH
P_LRG = P_DNG + DN_WIDTH
P_WIDTH = P_LRG + LRU_WIDTH
LANES = 128
GATE_SLOTS = LANES // DN_HEADS

N_CTX_CHUNKS = CTX_LEN // DN_CHUNK
N_SEQ_CHUNKS = SEQ // DN_CHUNK
N_SLOTS = N_CTX_CHUNKS + N_SEQ_CHUNKS
ALL_LEN = CTX_LEN + SEQ

LRU_SEGS = 8
NEG_BIG = -1e30

VMEM_LIMIT = 56 * 1024 * 1024


def _params(*sem):
    return pltpu.CompilerParams(dimension_semantics=sem, vmem_limit_bytes=VMEM_LIMIT)


def _mm(a, b):
    return jnp.dot(a, b, preferred_element_type=F32)


def _mo(v, m):
    return v if isinstance(v, int) else pl.multiple_of(v, m)


def _silu(x):
    return x * jax.nn.sigmoid(x)


def _sigmoid_tanh(x):
    return 0.5 * jnp.tanh(0.5 * x) + 0.5


def _softplus(x):
    return jnp.maximum(x, 0.0) + jnp.log1p(jnp.exp(-jnp.abs(x)))


def _mod_kernel(c_ref, w_ref, b_ref, o_ref):
    s = _silu(c_ref[...]).astype(BF16)
    o_ref[...] = _mm(s, w_ref[...].astype(BF16)) + b_ref[...]


def _modulation(c8, mod_w, mod_b):
    depth, dm, n = mod_w.shape
    tn = 512
    return pl.pallas_call(
        _mod_kernel,
        out_shape=jax.ShapeDtypeStruct((depth, 8, n), F32),
        grid=(depth, n // tn),
        in_specs=[
            pl.BlockSpec((8, dm), lambda l, j: (0, 0)),
            pl.BlockSpec((None, dm, tn), lambda l, j: (l, 0, j)),
            pl.BlockSpec((None, 1, tn), lambda l, j: (l, 0, j)),
        ],
        out_specs=pl.BlockSpec((None, 8, tn), lambda l, j: (l, 0, j)),
        compiler_params=_params("arbitrary", "arbitrary"),
        name="modulation",
    )(c8, mod_w, mod_b.reshape(depth, 1, n))


def _norm_rows(x_ref, nw_ref, sc_ref, sh_ref, hn_ref, rows_per_step=16):
    tm = x_ref.shape[0]
    nw = nw_ref[...]
    sc = 1.0 + sc_ref[...]
    sh = sh_ref[...]

    def body(r, _):
        rows = pl.ds(pl.multiple_of(r * rows_per_step, rows_per_step), rows_per_step)
        x = x_ref[rows, :]
        ms = jnp.mean(x * x, axis=-1, keepdims=True)
        y = x * lax.rsqrt(ms + EPS) * nw
        hn_ref[rows, :] = (y * sc + sh).astype(BF16)
        return 0

    lax.fori_loop(0, tm // rows_per_step, body, 0)


def _norm_mod_kernel(x_ref, c_ref, nw_ref, sc_ref, sh_ref, o_ref, *, n_x_tiles):
    @pl.when(pl.program_id(0) < n_x_tiles)
    def _():
        _norm_rows(x_ref, nw_ref, sc_ref, sh_ref, o_ref)

    @pl.when(pl.program_id(0) >= n_x_tiles)
    def _():
        _norm_rows(c_ref, nw_ref, sc_ref, sh_ref, o_ref)


def _norm_mod(x2d, c2d, nw, sc3, sh3, tm=512):
    m, dm = x2d.shape
    mc = 0 if c2d is None else c2d.shape[0]
    c2d = x2d if c2d is None else c2d
    n_x_tiles = m // tm
    per_batch = SEQ // tm
    mod_spec = pl.BlockSpec((None, 1, dm), lambda i: (i // per_batch, 0, 0))
    return pl.pallas_call(
        functools.partial(_norm_mod_kernel, n_x_tiles=n_x_tiles),
        out_shape=jax.ShapeDtypeStruct((m + mc, dm), BF16),
        grid=(n_x_tiles + mc // tm,),
        in_specs=[
            pl.BlockSpec((tm, dm), lambda i: (jnp.minimum(i, n_x_tiles - 1), 0)),
            pl.BlockSpec((tm, dm), lambda i: (jnp.maximum(i - n_x_tiles, 0), 0)),
            pl.BlockSpec((1, dm), lambda i: (0, 0)),
            mod_spec,
            mod_spec,
        ],
        out_specs=pl.BlockSpec((tm, dm), lambda i: (i, 0)),
        compiler_params=_params("arbitrary"),
        name="norm_mod",
    )(x2d, c2d, nw, sc3, sh3)


W_CAST_ROWS = 64


def _proj0_kernel(a_ref, w_ref, wn_ref, o_ref, wb_ref, *, first_shifted_tile, shift):
    j = pl.program_id(0)
    tn = w_ref.shape[1]

    def cast_rows(fn):
        def body(r, _):
            rows = pl.ds(pl.multiple_of(r * W_CAST_ROWS, W_CAST_ROWS), W_CAST_ROWS)
            wb_ref[rows, :] = fn(rows).astype(BF16)
            return 0

        lax.fori_loop(0, w_ref.shape[0] // W_CAST_ROWS, body, 0)

    @pl.when((pl.program_id(1) == 0) & (j < first_shifted_tile))
    def _():
        cast_rows(lambda rows: w_ref[rows, :])

    @pl.when((pl.program_id(1) == 0) & (j >= first_shifted_tile))
    def _():
        cast_rows(lambda rows: jnp.concatenate([w_ref[rows, :][:, shift:], wn_ref[rows, :][:, :shift]], axis=1))

    o_ref[...] = _mm(a_ref[...], wb_ref[...])


def _proj0(hn, w_in, tm=1088, tn=512):
    m, dm = hn.shape
    shift = AB_STATE - OFF_BETA
    return pl.pallas_call(
        functools.partial(_proj0_kernel, first_shifted_tile=OFF_BETA // tn, shift=shift),
        out_shape=jax.ShapeDtypeStruct((m, P_WIDTH), F32),
        grid=(P_WIDTH // tn, m // tm),
        in_specs=[
            pl.BlockSpec((tm, dm), lambda j, i: (i, 0)),
            pl.BlockSpec((dm, tn), lambda j, i: (0, j)),
            pl.BlockSpec((dm, LANES), lambda j, i: (0, (j + 1) * (tn // LANES))),
        ],
        out_specs=pl.BlockSpec((tm, tn), lambda j, i: (i, j)),
        scratch_shapes=[pltpu.VMEM((dm, tn), BF16)],
        compiler_params=_params("arbitrary", "arbitrary"),
        name="proj0",
    )(hn, w_in, w_in)


def _rows_matmul_kernel(a_ref, w_ref, o_ref):
    o_ref[...] = _mm(a_ref[...], w_ref[...])


def _rows_matmul(a, w, tm=512):
    m, dm = a.shape
    n = w.shape[1]
    return pl.pallas_call(
        _rows_matmul_kernel,
        out_shape=jax.ShapeDtypeStruct((m, n), F32),
        grid=(m // tm,),
        in_specs=[pl.BlockSpec((tm, dm), lambda i: (i, 0)), pl.BlockSpec((dm, n), lambda i: (0, 0))],
        out_specs=pl.BlockSpec((tm, n), lambda i: (i, 0)),
        compiler_params=_params("arbitrary"),
        name="gate_proj",
    )(a, w)


def _split3(a):
    a1 = a.astype(BF16)
    r1 = a - a1.astype(F32)
    a2 = r1.astype(BF16)
    a3 = (r1 - a2.astype(F32)).astype(BF16)
    return a1, a2, a3


def _gates_kernel(ba_ref, alog_ref, dt_ref, o_ref):
    tm = ba_ref.shape[0]
    r = lax.broadcasted_iota(jnp.int32, (DN_CHUNK, DN_CHUNK), 0)
    c = lax.broadcasted_iota(jnp.int32, (DN_CHUNK, DN_CHUNK), 1)
    lower = (r >= c).astype(BF16)
    upper = (r <= c).astype(BF16)
    ones = jnp.ones((DN_CHUNK, DN_CHUNK), BF16)
    slot = lax.broadcasted_iota(jnp.int32, (DN_CHUNK, LANES), 1) % GATE_SLOTS
    neg_a = -jnp.exp(alog_ref[...])
    dt = dt_ref[...]

    def body(i, _):
        rows = pl.ds(pl.multiple_of(i * DN_CHUNK, DN_CHUNK), DN_CHUNK)
        x = ba_ref[rows, :]
        beta = jax.nn.sigmoid(x)
        g = neg_a * _softplus(x + dt)
        g1, g2, g3 = _split3(g)
        cum = _mm(lower, g1) + _mm(lower, g2) + _mm(lower, g3)
        suf = _mm(upper, g1) + _mm(upper, g2) + _mm(upper, g3)
        tot = _mm(ones, g1) + _mm(ones, g2) + _mm(ones, g3)
        o_ref[rows, :] = jnp.where(slot < 2, beta, jnp.where(slot == 2, cum, jnp.where(slot == 3, suf, tot)))
        return 0

    lax.fori_loop(0, tm // DN_CHUNK, body, 0)


def _gates(ba, alog_l, dt_l, tm=512):
    m = ba.shape[0]
    return pl.pallas_call(
        _gates_kernel,
        out_shape=jax.ShapeDtypeStruct((m, LANES), F32),
        grid=(m // tm,),
        in_specs=[
            pl.BlockSpec((tm, LANES), lambda i: (i, 0)),
            pl.BlockSpec((1, LANES), lambda i: (0, 0)),
            pl.BlockSpec((1, LANES), lambda i: (0, 0)),
        ],
        out_specs=pl.BlockSpec((tm, LANES), lambda i: (i, 0)),
        compiler_params=_params("arbitrary"),
        name="dn_gates",
    )(ba, alog_l, dt_l)


HALO = 8


def _fill_halo_scratch(x_ref, xs_ref):
    length, width = x_ref.shape
    zeros = jnp.zeros((HALO, width), F32)
    xs_ref[pl.ds(0, HALO), :] = zeros
    xs_ref[pl.ds(HALO + length, HALO), :] = zeros

    def body(i, _):
        rows = pl.ds(pl.multiple_of(i * DN_CHUNK, DN_CHUNK), DN_CHUNK)
        xs_ref[pl.ds(pl.multiple_of(i * DN_CHUNK + HALO, HALO), DN_CHUNK), :] = x_ref[rows, :]
        return 0

    lax.fori_loop(0, length // DN_CHUNK, body, 0)


def _conv4_chunk(xs_ref, i, w):
    v = xs_ref[pl.ds(pl.multiple_of(i * DN_CHUNK, DN_CHUNK), DN_CHUNK + 2 * HALO), :]
    n = DN_CHUNK
    return (w[0:1] * v[HALO - 2:HALO - 2 + n] + w[1:2] * v[HALO - 1:HALO - 1 + n]
            + w[2:3] * v[HALO:HALO + n] + w[3:4] * v[HALO + 1:HALO + 1 + n])


def _dn_conv_kernel(x_ref, w_ref, o_ref, xs_ref, *, n_q_tiles, n_qk_tiles):
    j = pl.program_id(1)
    length, width = x_ref.shape
    _fill_halo_scratch(x_ref, xs_ref)
    w = w_ref[...]
    q_scale = jnp.where(j < n_q_tiles, DN_HEAD_DIM ** -0.5, 1.0)
    normed = j < n_qk_tiles

    def body(i, _):
        a = _silu(_conv4_chunk(xs_ref, i, w))
        parts = []
        for hh in range(width // DN_HEAD_DIM):
            ah = a[:, hh * DN_HEAD_DIM:(hh + 1) * DN_HEAD_DIM]
            ss = jnp.sum(ah * ah, axis=-1, keepdims=True)
            scale = jnp.where(normed, lax.rsqrt(ss + EPS) * q_scale, 1.0)
            parts.append(ah * scale)
        o_ref[pl.ds(pl.multiple_of(i * DN_CHUNK, DN_CHUNK), DN_CHUNK), :] = jnp.concatenate(parts, axis=1)
        return 0

    lax.fori_loop(0, length // DN_CHUNK, body, 0)


def _dn_conv(p2d, conv_w, length, row_blk0, n_tiles, w_tile0, n_q_tiles, n_qk_tiles, tc=256):
    nb = BATCH
    return pl.pallas_call(
        functools.partial(_dn_conv_kernel, n_q_tiles=n_q_tiles, n_qk_tiles=n_qk_tiles),
        out_shape=jax.ShapeDtypeStruct((nb * length, n_tiles * tc), F32),
        grid=(nb, n_tiles),
        in_specs=[
            pl.BlockSpec((length, tc), lambda b, j: (row_blk0 + b, j + w_tile0)),
            pl.BlockSpec((DN_CONV, tc), lambda b, j: (0, j + w_tile0)),
        ],
        out_specs=pl.BlockSpec((length, tc), lambda b, j: (b, j)),
        scratch_shapes=[pltpu.VMEM((length + 2 * HALO, tc), F32)],
        compiler_params=_params("arbitrary", "arbitrary"),
        name="dn_conv",
    )(p2d, conv_w)


def _split2(a):
    hi = a.astype(BF16)
    lo = (a - hi.astype(F32)).astype(BF16)
    return hi, lo


def _mm3(ah, al, bh, bl):
    return _mm(ah, bh) + _mm(ah, bl) + _mm(al, bh)


DN_GROUP = 8
N_TRI_LEVELS = 6
M_STRICT, M_INCL, M_EYE, M_JOIN = 0, 1, 2, 3


def _tri_masks():
    n = DN_CHUNK
    r = np.arange(n)[:, None].repeat(2 * n, axis=1)
    c = np.arange(2 * n)[None, :].repeat(n, axis=0) % n
    upper = np.arange(2 * n)[None, :] >= n
    rr = np.where(upper, c, r)
    cc = np.where(upper, r, c)
    out = [rr > cc, rr >= cc, rr == cc]
    for lvl in range(N_TRI_LEVELS):
        s = 1 << lvl
        out.append((rr // (2 * s) == cc // (2 * s)) & (rr % (2 * s) >= s) & (cc % (2 * s) < s))
    return jnp.asarray(np.stack(out).astype(np.float32))


def _block_diag2(x):
    lane = lax.broadcasted_iota(jnp.int32, x.shape, 1)
    zero = jnp.zeros_like(x)
    return jnp.concatenate([jnp.where(lane < DN_CHUNK, x, zero), jnp.where(lane >= DN_CHUNK, x, zero)], axis=0)


def _packed_tri_inverse(a_list, m_ref):
    d_list = [m_ref[M_EYE] - m_ref[M_JOIN] * a for a in a_list]
    for lvl in range(1, N_TRI_LEVELS):
        join = m_ref[M_JOIN + lvl]
        l_bd = [tuple(_block_diag2(p) for p in _split2(join * a)) for a in a_list]
        d_parts = [_split2(d) for d in d_list]
        e_parts = [_split2(_mm3(dh, dl, lh, ll)) for (dh, dl), (lh, ll) in zip(d_parts, l_bd)]
        d_bd = [(_block_diag2(dh), _block_diag2(dl)) for dh, dl in d_parts]
        d_list = [d - _mm3(eh, el, bh, bl) for d, (eh, el), (bh, bl) in zip(d_list, e_parts, d_bd)]
    return d_list


def _dn_local_group(h, srcs, g_all, gt_all, slot0, m_ref, nq_ref, c_ref, ge_ref, acc_ref):
    n = DN_CHUNK
    lane = lax.broadcasted_iota(jnp.int32, (n, LANES), 1)
    lane1 = lax.broadcasted_iota(jnp.int32, (1, LANES), 1)
    upper = lane >= n
    strict = m_ref[M_STRICT]
    incl = m_ref[M_INCL] > 0.5
    nt = (((1,), (1,)), ((), ()))
    cols, a_list, decays, kqs = [], [], [], []
    for e, (kc, vc, qc) in enumerate(srcs):
        g = g_all[e * n:(e + 1) * n]

        def col(s, g=g):
            return jnp.sum(jnp.where(lane == h * GATE_SLOTS + s, g, 0.0), axis=1, keepdims=True)

        cs = [col(s) for s in range(6)]
        t0 = (e // 2) * LANES
        row_f = gt_all[2:3, t0:t0 + LANES]
        row_b = gt_all[3:4, t0:t0 + LANES]
        if e % 2 == 0:
            g_row = jnp.where(lane1 < n, row_f, pltpu.roll(row_b, n, axis=1))
        else:
            g_row = jnp.where(lane1 < n, pltpu.roll(row_f, n, axis=1), row_b)
        b_pack = jnp.where(upper, cs[1], cs[0])
        g_pack = jnp.where(upper, cs[3], cs[2])
        decay = jnp.exp(jnp.where(incl, g_pack - g_row, NEG_BIG))
        kb = kc.astype(BF16)
        lhs = kb if qc is None else jnp.concatenate([kb, qc.astype(BF16)], axis=0)
        kq = lax.dot_general(lhs, jnp.concatenate([kb, kb], axis=0), nt, preferred_element_type=F32)
        cols.append(cs)
        decays.append(decay)
        kqs.append(kq)
        a_list.append(strict * (b_pack * kq[:n] * decay))
    t_inv = _packed_tri_inverse(a_list, m_ref)
    for e, (kc, vc, qc) in enumerate(srcs):
        b_f, b_b, g_f, g_b, tot_f, tot_b = cols[e]
        eg = (jnp.exp(g_f), jnp.exp(g_b))
        rhs = jnp.concatenate([jnp.concatenate([kc * (b_f * eg[0]), vc * b_f], axis=1),
                               jnp.concatenate([kc * (b_b * eg[1]), vc * b_b], axis=1)], axis=0)
        th, tl = (_block_diag2(p) for p in _split2(t_inv[e]))
        rh, rl = _split2(rhs)
        wu = _mm3(th, tl, rh, rl)
        slot = slot0 + e
        nq_rows = _mo(slot * 3 * n, n)
        for d in range(2):
            g_col, tot_col = (g_f, tot_f) if d == 0 else (g_b, tot_b)
            ke_t = (kc * jnp.exp(tot_col - g_col)).T.astype(BF16)
            wub = wu[d * n:(d + 1) * n].astype(BF16)
            ge_ref[d, pl.ds(_mo(slot * 8, 8), 8), :] = jnp.broadcast_to(jnp.exp(tot_col[:8]), (8, LANES))
            if qc is None:
                nc = _mm(ke_t, wub)
            else:
                a_qk = (kqs[e][n:] * decays[e])[:, d * n:(d + 1) * n].astype(BF16)
                nc = _mm(jnp.concatenate([ke_t, a_qk], axis=0), wub)
                nq_ref[d, pl.ds(nq_rows + 2 * n, n), :] = (qc * eg[d] - nc[2 * n:, :DN_HEAD_DIM]).astype(BF16)
                rows = pl.ds(_mo((slot - N_CTX_CHUNKS) * n, n), n)
                acc_ref[rows, :] += nc[2 * n:, DN_HEAD_DIM:]
            nq_ref[d, pl.ds(nq_rows, 2 * n), :] = nc[:2 * n, :DN_HEAD_DIM].astype(BF16)
            c_ref[d, pl.ds(_mo(slot * 2 * n, 2 * n), 2 * n), :] = nc[:2 * n, DN_HEAD_DIM:]


def _dn_state_step(d, slot, s, nq_ref, c_ref, ge_ref, with_output):
    n = DN_CHUNK
    nq_rows = _mo(slot * 3 * n, n)
    ge = jnp.tile(ge_ref[d, pl.ds(_mo(slot * 8, 8), 8), :], (DN_HEAD_DIM // 8, 1))
    c = c_ref[d, pl.ds(_mo(slot * 2 * n, 2 * n), 2 * n), :]
    r = _mm(nq_ref[d, pl.ds(nq_rows, (3 if with_output else 2) * n), :], s.astype(BF16))
    return s * ge - r[:2 * n] + c, (r[2 * n:] if with_output else None)


def _deltanet_kernel(q_ref, k_ref, v_ref, kc_ref, vc_ref, g_ref, gt_ref, gate_ref, nw_ref, m_ref, o_ref,
                     nq_ref, c_ref, ge_ref, acc_ref):
    h = pl.program_id(1)
    n = DN_CHUNK
    gl = DN_GROUP * n
    acc_ref[...] = jnp.zeros_like(acc_ref)

    srcs = [(kc_ref[pl.ds(e * n, n), :], vc_ref[pl.ds(e * n, n), :], None) for e in range(N_CTX_CHUNKS)]
    _dn_local_group(h, srcs, g_ref[pl.ds(0, CTX_LEN), :], gt_ref[:, pl.ds(0, CTX_LEN)], 0,
                    m_ref, nq_ref, c_ref, ge_ref, acc_ref)

    def local_body(p, _):
        base = pl.multiple_of(p * gl, gl)
        gbase = pl.multiple_of(CTX_LEN + p * gl, gl)
        srcs = [(k_ref[pl.ds(base + e * n, n), :], v_ref[pl.ds(base + e * n, n), :], q_ref[pl.ds(base + e * n, n), :])
                for e in range(DN_GROUP)]
        _dn_local_group(h, srcs, g_ref[pl.ds(gbase, gl), :], gt_ref[:, pl.ds(gbase, gl)], N_CTX_CHUNKS + p * DN_GROUP,
                        m_ref, nq_ref, c_ref, ge_ref, acc_ref)
        return 0

    lax.fori_loop(0, N_SEQ_CHUNKS // DN_GROUP, local_body, 0)

    s_f = jnp.zeros((DN_HEAD_DIM, DN_HEAD_DIM), F32)
    s_b = jnp.zeros((DN_HEAD_DIM, DN_HEAD_DIM), F32)
    for i in range(N_CTX_CHUNKS):
        s_f, _ = _dn_state_step(0, i, s_f, nq_ref, c_ref, ge_ref, False)
        s_b, _ = _dn_state_step(1, N_CTX_CHUNKS - 1 - i, s_b, nq_ref, c_ref, ge_ref, False)

    def state_body(i, carry):
        s_f, s_b = carry
        s_f, o_f = _dn_state_step(0, N_CTX_CHUNKS + i, s_f, nq_ref, c_ref, ge_ref, True)
        rows_f = pl.ds(pl.multiple_of(i * n, n), n)
        acc_ref[rows_f, :] += o_f
        jb = N_SEQ_CHUNKS - 1 - i
        s_b, o_b = _dn_state_step(1, N_CTX_CHUNKS + jb, s_b, nq_ref, c_ref, ge_ref, True)
        rows_b = pl.ds(pl.multiple_of(jb * n, n), n)
        acc_ref[rows_b, :] += o_b
        return s_f, s_b

    lax.fori_loop(0, N_SEQ_CHUNKS, state_body, (s_f, s_b))

    nw = nw_ref[...]

    def out_body(i, _):
        rows = pl.ds(pl.multiple_of(i * n, n), n)
        o = acc_ref[rows, :]
        ms = jnp.mean(o * o, axis=-1, keepdims=True)
        o_ref[rows, :] = (o * lax.rsqrt(ms + EPS) * nw * _silu(gate_ref[rows, :])).astype(BF16)
        return 0

    lax.fori_loop(0, N_SEQ_CHUNKS, out_body, 0)


def _deltanet(qkvn, kvn_c, gates_tok, gates_t, proj, dn_norm):
    n = DN_CHUNK
    hb = DN_WIDTH // DN_HEAD_DIM
    return pl.pallas_call(
        _deltanet_kernel,
        out_shape=jax.ShapeDtypeStruct((BATCH * SEQ, DN_WIDTH), BF16),
        grid=(BATCH, DN_HEADS),
        in_specs=[
            pl.BlockSpec((SEQ, DN_HEAD_DIM), lambda b, h: (b, h)),
            pl.BlockSpec((SEQ, DN_HEAD_DIM), lambda b, h: (b, hb + h)),
            pl.BlockSpec((SEQ, DN_HEAD_DIM), lambda b, h: (b, 2 * hb + h)),
            pl.BlockSpec((CTX_LEN, DN_HEAD_DIM), lambda b, h: (b, h)),
            pl.BlockSpec((CTX_LEN, DN_HEAD_DIM), lambda b, h: (b, hb + h)),
            pl.BlockSpec((None, ALL_LEN, LANES), lambda b, h: (b, 0, 0)),
            pl.BlockSpec((None, GATE_SLOTS, ALL_LEN), lambda b, h: (b, h, 0)),
            pl.BlockSpec((SEQ, DN_HEAD_DIM), lambda b, h: (b, P_DNG // DN_HEAD_DIM + h)),
            pl.BlockSpec((1, DN_HEAD_DIM), lambda b, h: (0, 0)),
            pl.BlockSpec((M_JOIN + N_TRI_LEVELS, n, 2 * n), lambda b, h: (0, 0, 0)),
        ],
        out_specs=pl.BlockSpec((SEQ, DN_HEAD_DIM), lambda b, h: (b, h)),
        scratch_shapes=[
            pltpu.VMEM((2, N_SLOTS * 3 * n, DN_HEAD_DIM), BF16),
            pltpu.VMEM((2, N_SLOTS * 2 * n, DN_HEAD_DIM), F32),
            pltpu.VMEM((2, N_SLOTS * 8, LANES), F32),
            pltpu.VMEM((SEQ, DN_HEAD_DIM), F32),
        ],
        compiler_params=_params("arbitrary", "arbitrary"),
        name="deltanet",
    )(qkvn, qkvn, qkvn, kvn_c, kvn_c, gates_tok, gates_t, proj, dn_norm, _tri_masks())


def _shift_down(y):
    row = lax.broadcasted_iota(jnp.int32, y.shape, 0)
    return jnp.where(row == 0, 0.0, pltpu.roll(y, 1, axis=0))


def _shift_up(y):
    row = lax.broadcasted_iota(jnp.int32, y.shape, 0)
    return jnp.where(row == y.shape[0] - 1, 0.0, pltpu.roll(y, y.shape[0] - 1, axis=0))


def _lru_gate_rows(xc, wcat, bcat, sp, a_ref, b_ref, rows):
    g = _mm(xc.astype(BF16), wcat) + bcat
    w = LRU_BLOCK_DIM
    for d in range(2):
        r = _sigmoid_tanh(g[:, (2 * d) * w:(2 * d + 1) * w])
        i = _sigmoid_tanh(g[:, (2 * d + 1) * w:(2 * d + 2) * w])
        a = jnp.exp(-LRU_C * r * sp[d:d + 1, :])
        a_ref[d, rows, :] = a
        b_ref[d, rows, :] = jnp.sqrt(1.0 - a * a) * (i * xc)


def _seg_rows(j, steps, stride_rows, colmajor):
    if colmajor:
        start = (j % GRID_ROWS) * GRID_W + j // GRID_ROWS
        return pl.ds(start, LRU_SEGS, stride=GRID_W // LRU_SEGS)
    return pl.ds(j, LRU_SEGS, stride=steps)


def _carry_chain(h_end, p_end, c0, reverse):
    row = lax.broadcasted_iota(jnp.int32, h_end.shape, 0)
    carries = jnp.zeros_like(h_end)
    c = c0
    order = range(LRU_SEGS - 1, -1, -1) if reverse else range(LRU_SEGS)
    for gseg in order:
        carries = jnp.where(row == gseg, c, carries)
        c = h_end[gseg:gseg + 1, :] + p_end[gseg:gseg + 1, :] * c
    return carries, c


def _lru_kernel(x_ref, xc_in_ref, gate_ref, cw_ref, cb_ref, wcat_ref, bcat_ref, lam_ref, o_ref,
                xc_ref, xs_ref, xcc_ref, a_ref, b_ref, ac_ref, bc_ref, hl_ref, pp_ref, hs_ref):
    cw = cw_ref[...]
    cb = cb_ref[...]
    wcat = wcat_ref[...]
    bcat = bcat_ref[...]
    sp = _softplus(-lam_ref[...])
    gw = GRID_W

    _fill_halo_scratch(xc_in_ref, xs_ref)
    for i in range(CTX_LEN // DN_CHUNK):
        rows = pl.ds(i * DN_CHUNK, DN_CHUNK)
        xcc_ref[rows, :] = _conv4_chunk(xs_ref, i, cw) + cb
    _lru_gate_rows(xcc_ref[...], wcat, bcat, sp, ac_ref, bc_ref, pl.ds(0, CTX_LEN))

    csteps = CTX_LEN // LRU_SEGS
    zero = jnp.zeros((LRU_SEGS, LRU_BLOCK_DIM), F32)
    one = jnp.ones((LRU_SEGS, LRU_BLOCK_DIM), F32)

    def ctx_body(j, carry):
        hf, pf, hb, pb = carry
        rf = _seg_rows(j, csteps, None, False)
        a = ac_ref[0, rf, :]
        hf = a * hf + bc_ref[0, rf, :]
        pf = a * pf
        rb = _seg_rows(csteps - 1 - j, csteps, None, False)
        a = ac_ref[1, rb, :]
        hb = a * hb + bc_ref[1, rb, :]
        pb = a * pb
        return hf, pf, hb, pb

    hf, pf, hb, pb = lax.fori_loop(0, csteps, ctx_body, (zero, one, zero, one))
    zrow = jnp.zeros((1, LRU_BLOCK_DIM), F32)
    _, h0_f = _carry_chain(hf, pf, zrow, False)
    _, h0_b = _carry_chain(hb, pb, zrow, True)

    def slab(r):
        return x_ref[pl.ds(pl.multiple_of(r * gw, gw), gw), :]

    def conv_slab(r, xm2, xm1, x0, xp1):
        xc_ref[pl.ds(pl.multiple_of(r * gw, gw), gw), :] = (
            cb + cw[0:1] * xm2 + cw[1:2] * xm1 + cw[2:3] * x0 + cw[3:4] * xp1)

    last = GRID_ROWS - 1
    conv_slab(0, _shift_down(slab(last - 1)), _shift_down(slab(last)), slab(0), slab(1))
    conv_slab(1, _shift_down(slab(last)), slab(0), slab(1), slab(2))
    conv_slab(last, slab(last - 2), slab(last - 1), slab(last), _shift_up(slab(0)))

    def conv_body(r, _):
        conv_slab(r, slab(r - 2), slab(r - 1), slab(r), slab(r + 1))
        return 0

    lax.fori_loop(2, last, conv_body, 0)

    gate_rows = 256

    def gate_body(i, _):
        rows = pl.ds(pl.multiple_of(i * gate_rows, gate_rows), gate_rows)
        _lru_gate_rows(xc_ref[rows, :], wcat, bcat, sp, a_ref, b_ref, rows)
        return 0

    lax.fori_loop(0, SEQ // gate_rows, gate_body, 0)

    steps = SEQ // LRU_SEGS

    seg_stride = GRID_W // LRU_SEGS
    unroll = 8

    def seg_rows(wq, r):
        return pl.ds(r * GRID_W + wq, LRU_SEGS, stride=seg_stride)

    def scan_col(wq, carry):
        def scan_rows(rr, carry):
            hf, pf, hb, pb = carry
            for k in range(unroll):
                r = rr * unroll + k
                rf = seg_rows(wq, r)
                a = a_ref[0, rf, :]
                hf = a * hf + b_ref[0, rf, :]
                pf = a * pf
                hl_ref[0, wq * GRID_ROWS + r] = hf
                pp_ref[0, wq * GRID_ROWS + r] = pf
                wqb = seg_stride - 1 - wq
                rb = GRID_ROWS - 1 - r
                rbk = seg_rows(wqb, rb)
                a = a_ref[1, rbk, :]
                hb = a * hb + b_ref[1, rbk, :]
                pb = a * pb
                hl_ref[1, wqb * GRID_ROWS + rb] = hb
                pp_ref[1, wqb * GRID_ROWS + rb] = pb
            return hf, pf, hb, pb

        return lax.fori_loop(0, GRID_ROWS // unroll, scan_rows, carry)

    hf, pf, hb, pb = lax.fori_loop(0, seg_stride, scan_col, (zero, one, zero, one))
    c_f, _ = _carry_chain(hf, pf, h0_f, False)
    c_b, _ = _carry_chain(hb, pb, h0_b, True)

    def fix_col(wq, _):
        def fix_rows(rr, _):
            for k in range(unroll):
                r = rr * unroll + k
                j = wq * GRID_ROWS + r
                h = hl_ref[0, j] + pp_ref[0, j] * c_f + hl_ref[1, j] + pp_ref[1, j] * c_b
                hs_ref[seg_rows(wq, r), :] = h
            return 0

        return lax.fori_loop(0, GRID_ROWS // unroll, fix_rows, 0)

    lax.fori_loop(0, seg_stride, fix_col, 0)

    def out_body(i, _):
        rows = pl.ds(pl.multiple_of(i * gate_rows, gate_rows), gate_rows)
        o_ref[rows, :] = (hs_ref[rows, :] * _silu(gate_ref[rows, :])).astype(BF16)
        return 0

    lax.fori_loop(0, SEQ // gate_rows, out_body, 0)


def _lru(proj, conv_w, conv_b, wcat, bcat, lam):
    w = LRU_BLOCK_DIM
    steps = SEQ // LRU_SEGS
    ctx_blk0 = BATCH * SEQ // CTX_LEN
    return pl.pallas_call(
        _lru_kernel,
        out_shape=jax.ShapeDtypeStruct((BATCH * SEQ, LRU_WIDTH), BF16),
        grid=(BATCH, LRU_BLOCKS),
        in_specs=[
            pl.BlockSpec((SEQ, w), lambda b, n: (b, P_LRU // w + n)),
            pl.BlockSpec((CTX_LEN, w), lambda b, n: (ctx_blk0 + b, P_LRU // w + n)),
            pl.BlockSpec((SEQ, w), lambda b, n: (b, P_LRG // w + n)),
            pl.BlockSpec((LRU_CONV, w), lambda b, n: (0, n)),
            pl.BlockSpec((1, w), lambda b, n: (0, n)),
            pl.BlockSpec((None, w, 4 * w), lambda b, n: (n, 0, 0)),
            pl.BlockSpec((None, 1, 4 * w), lambda b, n: (n, 0, 0)),
            pl.BlockSpec((None, 2, w), lambda b, n: (n, 0, 0)),
        ],
        out_specs=pl.BlockSpec((SEQ, w), lambda b, n: (b, n)),
        scratch_shapes=[
            pltpu.VMEM((SEQ, w), F32),
            pltpu.VMEM((CTX_LEN + 2 * HALO, w), F32),
            pltpu.VMEM((CTX_LEN, w), F32),
            pltpu.VMEM((2, SEQ, w), F32),
            pltpu.VMEM((2, SEQ, w), F32),
            pltpu.VMEM((2, CTX_LEN, w), F32),
            pltpu.VMEM((2, CTX_LEN, w), F32),
            pltpu.VMEM((2, steps, LRU_SEGS, w), F32),
            pltpu.VMEM((2, steps, LRU_SEGS, w), F32),
            pltpu.VMEM((SEQ, w), F32),
        ],
        compiler_params=_params("arbitrary", "arbitrary"),
        name="rglru",
    )(proj, proj, proj, conv_w, conv_b, wcat, bcat, lam)


def _out_proj_kernel(ya_ref, yb_ref, wa_ref, wb_ref, x_ref, g_ref, o_ref, wcast_ref):
    @pl.when(pl.program_id(1) == 0)
    def _():
        def cast_body(r, _):
            rows = pl.ds(pl.multiple_of(r * W_CAST_ROWS, W_CAST_ROWS), W_CAST_ROWS)
            wcast_ref[0, rows, :] = wa_ref[rows, :].astype(BF16)
            wcast_ref[1, rows, :] = wb_ref[rows, :].astype(BF16)
            return 0

        lax.fori_loop(0, wa_ref.shape[0] // W_CAST_ROWS, cast_body, 0)

    acc = _mm(ya_ref[...], wcast_ref[0]) + _mm(yb_ref[...], wcast_ref[1])
    o_ref[...] = x_ref[...] + g_ref[...] * acc


def _out_proj(ya, yb, ya_blk, yb_blk, w, x2d, gate3, tm=1024, tn=512):
    m, dm = x2d.shape
    kh = w.shape[0] // 2
    per_batch = SEQ // tm
    return pl.pallas_call(
        _out_proj_kernel,
        out_shape=jax.ShapeDtypeStruct((m, dm), F32),
        grid=(dm // tn, m // tm),
        in_specs=[
            pl.BlockSpec((tm, kh), lambda j, i: (i, ya_blk)),
            pl.BlockSpec((tm, kh), lambda j, i: (i, yb_blk)),
            pl.BlockSpec((kh, tn), lambda j, i: (0, j)),
            pl.BlockSpec((kh, tn), lambda j, i: (1, j)),
            pl.BlockSpec((tm, tn), lambda j, i: (i, j)),
            pl.BlockSpec((None, 1, tn), lambda j, i: (i // per_batch, 0, j)),
        ],
        out_specs=pl.BlockSpec((tm, tn), lambda j, i: (i, j)),
        scratch_shapes=[pltpu.VMEM((2, kh, tn), BF16)],
        compiler_params=_params("arbitrary", "arbitrary"),
        name="out_proj",
    )(ya, yb, w, w, x2d, gate3)


def _sc_proj_kernel(a_ref, wb_ref, wc_ref, wx_ref, wg_ref, cw_ref, o_ref, wcast_ref, z_ref, bg_ref):
    @pl.when(pl.program_id(1) == 0)
    def _():
        def cast_body(r, _):
            rows = pl.ds(pl.multiple_of(r * W_CAST_ROWS, W_CAST_ROWS), W_CAST_ROWS)
            for gi, w_ref in enumerate((wb_ref, wc_ref, wx_ref, wg_ref)):
                wcast_ref[gi, rows, :] = w_ref[rows, :].astype(BF16)
            return 0

        lax.fori_loop(0, wb_ref.shape[0] // W_CAST_ROWS, cast_body, 0)

    hn = a_ref[...]
    z_ref[...] = _mm(hn, wcast_ref[1]) * _mm(hn, wcast_ref[2])
    bg_ref[...] = _mm(hn, wcast_ref[0]) * _silu(_mm(hn, wcast_ref[3]))
    cw = cw_ref[...]
    gw = GRID_W

    def body(r, _):
        rows = pl.ds(pl.multiple_of(r * gw, gw), gw)
        z = z_ref[rows, :]
        zc = cw[0:1] * _shift_down(z) + cw[1:2] * z + cw[2:3] * _shift_up(z)
        o_ref[rows, :] = (bg_ref[rows, :] * zc).astype(BF16)
        return 0

    lax.fori_loop(0, a_ref.shape[0] // gw, body, 0)


def _sc_proj(hn, w, conv_w, tm=512, tc=256):
    m, dm = hn.shape
    nt = SC_WIDTH // tc
    return pl.pallas_call(
        _sc_proj_kernel,
        out_shape=jax.ShapeDtypeStruct((m, SC_WIDTH), BF16),
        grid=(nt, m // tm),
        in_specs=[
            pl.BlockSpec((tm, dm), lambda j, i: (i, 0)),
            pl.BlockSpec((dm, tc), lambda j, i: (0, j)),
            pl.BlockSpec((dm, tc), lambda j, i: (0, nt + j)),
            pl.BlockSpec((dm, tc), lambda j, i: (0, 2 * nt + j)),
            pl.BlockSpec((dm, tc), lambda j, i: (0, 3 * nt + j)),
            pl.BlockSpec((SC_CONV, tc), lambda j, i: (0, j)),
        ],
        out_specs=pl.BlockSpec((tm, tc), lambda j, i: (i, j)),
        scratch_shapes=[pltpu.VMEM((4, dm, tc), BF16), pltpu.VMEM((tm, tc), F32), pltpu.VMEM((tm, tc), F32)],
        compiler_params=_params("arbitrary", "arbitrary"),
        name="sc_proj",
    )(hn, w, w, w, w, conv_w)


def _final_norm_kernel(x_ref, w_ref, o_ref):
    w = w_ref[...]

    def body(r, _):
        rows = pl.ds(pl.multiple_of(r * 16, 16), 16)
        x = x_ref[rows, :]
        ms = jnp.mean(x * x, axis=-1, keepdims=True)
        o_ref[rows, :] = x * lax.rsqrt(ms + EPS) * w
        return 0

    lax.fori_loop(0, x_ref.shape[0] // 16, body, 0)


def _final_norm(x2d, w, tm=256):
    m, dm = x2d.shape
    return pl.pallas_call(
        _final_norm_kernel,
        out_shape=jax.ShapeDtypeStruct((m, dm), F32),
        grid=(m // tm,),
        in_specs=[pl.BlockSpec((tm, dm), lambda i: (i, 0)), pl.BlockSpec((1, dm), lambda i: (0, 0))],
        out_specs=pl.BlockSpec((tm, dm), lambda i: (i, 0)),
        compiler_params=_params("arbitrary"),
        name="final_norm",
    )(x2d, w)


def _gate_lane_vector(p):
    z = jnp.zeros((DN_HEADS,), F32)
    cols = jnp.stack([z, z, p[0], p[1], p[0], p[1], z, z], axis=1)
    return cols.reshape(1, LANES).astype(F32)


def _gate_weight(w_in):
    h = jnp.arange(DN_HEADS)
    idx = jnp.stack([OFF_BETA + h, OFF_BETA + DN_HEADS + h, OFF_ALPHA + h, OFF_ALPHA + DN_HEADS + h,
                     OFF_ALPHA + h, OFF_ALPHA + DN_HEADS + h, OFF_BETA + h, OFF_BETA + h], axis=1).reshape(-1)
    live = (jnp.arange(LANES) % GATE_SLOTS) < 6
    return jnp.where(live[None, :], w_in[:, idx], 0.0).astype(BF16)


def kernel(x, c, ctx, c_ctx, mod_w, mod_b, norm_w, ab_w_in, ab_qkv_conv, ab_a_log, ab_dt_bias, ab_dn_norm,
           ab_lru_conv_w, ab_lru_conv_b, ab_lru_w_r, ab_lru_b_r, ab_lru_w_i, ab_lru_b_i, ab_lru_lambda, ab_w_out,
           sc_w_in, sc_conv, sc_w_out, final_norm_w):
    dm = D_MODEL
    x2d = x.reshape(BATCH * SEQ, dm)
    ctx2d = ctx.reshape(BATCH * CTX_LEN, dm)

    c8 = jnp.concatenate([c, c_ctx[None, :], jnp.zeros((8 - BATCH - 1, dm), F32)], axis=0)
    mod = _modulation(c8, mod_w, mod_b)
    shift = [mod[l, :, :dm].reshape(8, 1, dm) for l in range(2)]
    scale = [mod[l, :, dm:2 * dm].reshape(8, 1, dm) for l in range(2)]
    gate = [mod[l, :, 2 * dm:].reshape(8, 1, dm) for l in range(2)]

    w_in = ab_w_in[0]
    nw0 = norm_w[0].reshape(1, dm)
    hn = _norm_mod(x2d, ctx2d, nw0, scale[0], shift[0])
    proj = _proj0(hn, w_in)
    ba = _rows_matmul(hn, _gate_weight(w_in))

    n_lat = BATCH * SEQ
    ba_all = jnp.concatenate([ba[n_lat:].reshape(BATCH, CTX_LEN, LANES), ba[:n_lat].reshape(BATCH, SEQ, LANES)], axis=1)
    gates_tok = _gates(ba_all.reshape(BATCH * ALL_LEN, LANES), _gate_lane_vector(ab_a_log[0]),
                       _gate_lane_vector(ab_dt_bias[0])).reshape(BATCH, ALL_LEN, LANES)
    gates_t = gates_tok.transpose(0, 2, 1)

    qkv_conv = ab_qkv_conv[0]
    tc = 256
    qkvn = _dn_conv(proj, qkv_conv, SEQ, 0, 3 * DN_WIDTH // tc, 0, DN_WIDTH // tc, 2 * DN_WIDTH // tc, tc)
    kvn_c = _dn_conv(proj, qkv_conv, CTX_LEN, n_lat // CTX_LEN, 2 * DN_WIDTH // tc, DN_WIDTH // tc, 0, DN_WIDTH // tc, tc)
    y_dn = _deltanet(qkvn, kvn_c, gates_tok, gates_t, proj, ab_dn_norm[0].reshape(1, DN_HEAD_DIM))

    wcat = jnp.concatenate([ab_lru_w_r[0, 0], ab_lru_w_i[0, 0], ab_lru_w_r[0, 1], ab_lru_w_i[0, 1]],
                           axis=-1).astype(BF16)
    bl = LRU_BLOCK_DIM
    bcat = jnp.concatenate([ab_lru_b_r[0, 0].reshape(LRU_BLOCKS, 1, bl), ab_lru_b_i[0, 0].reshape(LRU_BLOCKS, 1, bl),
                            ab_lru_b_r[0, 1].reshape(LRU_BLOCKS, 1, bl), ab_lru_b_i[0, 1].reshape(LRU_BLOCKS, 1, bl)],
                           axis=-1)
    lam = ab_lru_lambda[0].reshape(2, LRU_BLOCKS, bl).transpose(1, 0, 2)
    y_lru = _lru(proj, ab_lru_conv_w[0], ab_lru_conv_b[0].reshape(1, LRU_WIDTH), wcat, bcat, lam)

    x1 = _out_proj(y_dn, y_lru, 0, 0, ab_w_out[0], x2d, gate[0])

    hn1 = _norm_mod(x1, None, norm_w[1].reshape(1, dm), scale[1], shift[1])
    y1 = _sc_proj(hn1, sc_w_in[0], sc_conv[0])
    x2 = _out_proj(y1, y1, 0, 1, sc_w_out[0], x1, gate[1])

    return _final_norm(x2, final_norm_w.reshape(1, dm)).reshape(BATCH, SEQ, dm)
```

```python
import functools

import jax
import jax.numpy as jnp
import numpy as np
from jax import lax
from jax.experimental import pallas as pl
from jax.experimental.pallas import tpu as pltpu

F32 = jnp.float32
BF16 = jnp.bfloat16

D_MODEL = 4096
BATCH = 2
SEQ = 4096
CTX_LEN = 256
GRID_W = 64
GRID_ROWS = SEQ // GRID_W
EPS = 1e-6

DN_HEADS = 16
DN_HEAD_DIM = 128
DN_WIDTH = DN_HEADS * DN_HEAD_DIM
DN_CONV = 4
DN_CHUNK = 64

LRU_WIDTH = 2048
LRU_BLOCKS = 16
LRU_BLOCK_DIM = LRU_WIDTH // LRU_BLOCKS
LRU_CONV = 4
LRU_C = 8.0

SC_WIDTH = D_MODEL
SC_CONV = 3

OFF_LRU = 3 * DN_WIDTH
OFF_BETA = OFF_LRU + LRU_WIDTH
OFF_ALPHA = OFF_BETA + 2 * DN_HEADS
AB_STATE = OFF_ALPHA + 2 * DN_HEADS

P_K = DN_WIDTH
P_V = 2 * DN_WIDTH
P_LRU = 3 * DN_WIDTH
P_DNG = P_LRU + LRU_WIDTH
P_LRG = P_DNG + DN_WIDTH
P_WIDTH = P_LRG + LRU_WIDTH
LANES = 128
GATE_SLOTS = LANES // DN_HEADS

N_CTX_CHUNKS = CTX_LEN // DN_CHUNK
N_SEQ_CHUNKS = SEQ // DN_CHUNK
N_SLOTS = N_CTX_CHUNKS + N_SEQ_CHUNKS
ALL_LEN = CTX_LEN + SEQ

LRU_SEGS = 8
NEG_BIG = -1e30

VMEM_LIMIT = 56 * 1024 * 1024


def _params(*sem):
    return pltpu.CompilerParams(dimension_semantics=sem, vmem_limit_bytes=VMEM_LIMIT)


def _mm(a, b):
    return jnp.dot(a, b, preferred_element_type=F32)


def _mo(v, m):
    return v if isinstance(v, int) else pl.multiple_of(v, m)


def _silu(x):
    return x * jax.nn.sigmoid(x)


def _sigmoid_tanh(x):
    return 0.5 * jnp.tanh(0.5 * x) + 0.5


def _softplus(x):
    return jnp.maximum(x, 0.0) + jnp.log1p(jnp.exp(-jnp.abs(x)))


def _mod_kernel(c_ref, w_ref, b_ref, o_ref):
    s = _silu(c_ref[...]).astype(BF16)
    o_ref[...] = _mm(s, w_ref[...].astype(BF16)) + b_ref[...]


def _modulation(c8, mod_w, mod_b):
    depth, dm, n = mod_w.shape
    tn = 512
    return pl.pallas_call(
        _mod_kernel,
        out_shape=jax.ShapeDtypeStruct((depth, 8, n), F32),
        grid=(depth, n // tn),
        in_specs=[
            pl.BlockSpec((8, dm), lambda l, j: (0, 0)),
            pl.BlockSpec((None, dm, tn), lambda l, j: (l, 0, j)),
            pl.BlockSpec((None, 1, tn), lambda l, j: (l, 0, j)),
        ],
        out_specs=pl.BlockSpec((None, 8, tn), lambda l, j: (l, 0, j)),
        compiler_params=_params("arbitrary", "arbitrary"),
        name="modulation",
    )(c8, mod_w, mod_b.reshape(depth, 1, n))


def _norm_rows(x_ref, nw_ref, sc_ref, sh_ref, hn_ref, rows_per_step=16):
    tm = x_ref.shape[0]
    nw = nw_ref[...]
    sc = 1.0 + sc_ref[...]
    sh = sh_ref[...]

    def body(r, _):
        rows = pl.ds(pl.multiple_of(r * rows_per_step, rows_per_step), rows_per_step)
        x = x_ref[rows, :]
        ms = jnp.mean(x * x, axis=-1, keepdims=True)
        y = x * lax.rsqrt(ms + EPS) * nw
        hn_ref[rows, :] = (y * sc + sh).astype(BF16)
        return 0

    lax.fori_loop(0, tm // rows_per_step, body, 0)


def _norm_mod_kernel(x_ref, c_ref, nw_ref, sc_ref, sh_ref, o_ref, *, n_x_tiles):
    @pl.when(pl.program_id(0) < n_x_tiles)
    def _():
        _norm_rows(x_ref, nw_ref, sc_ref, sh_ref, o_ref)

    @pl.when(pl.program_id(0) >= n_x_tiles)
    def _():
        _norm_rows(c_ref, nw_ref, sc_ref, sh_ref, o_ref)


def _norm_mod(x2d, c2d, nw, sc3, sh3, tm=512):
    m, dm = x2d.shape
    mc = 0 if c2d is None else c2d.shape[0]
    c2d = x2d if c2d is None else c2d
    n_x_tiles = m // tm
    per_batch = SEQ // tm
    mod_spec = pl.BlockSpec((None, 1, dm), lambda i: (i // per_batch, 0, 0))
    return pl.pallas_call(
        functools.partial(_norm_mod_kernel, n_x_tiles=n_x_tiles),
        out_shape=jax.ShapeDtypeStruct((m + mc, dm), BF16),
        grid=(n_x_tiles + mc // tm,),
        in_specs=[
            pl.BlockSpec((tm, dm), lambda i: (jnp.minimum(i, n_x_tiles - 1), 0)),
            pl.BlockSpec((tm, dm), lambda i: (jnp.maximum(i - n_x_tiles, 0), 0)),
            pl.BlockSpec((1, dm), lambda i: (0, 0)),
            mod_spec,
            mod_spec,
        ],
        out_specs=pl.BlockSpec((tm, dm), lambda i: (i, 0)),
        compiler_params=_params("arbitrary"),
        name="norm_mod",
    )(x2d, c2d, nw, sc3, sh3)


W_CAST_ROWS = 64


def _proj0_kernel(a_ref, w_ref, wn_ref, o_ref, wb_ref, *, first_shifted_tile, shift):
    j = pl.program_id(0)
    tn = w_ref.shape[1]

    def cast_rows(fn):
        def body(r, _):
            rows = pl.ds(pl.multiple_of(r * W_CAST_ROWS, W_CAST_ROWS), W_CAST_ROWS)
            wb_ref[rows, :] = fn(rows).astype(BF16)
            return 0

        lax.fori_loop(0, w_ref.shape[0] // W_CAST_ROWS, body, 0)

    @pl.when((pl.program_id(1) == 0) & (j < first_shifted_tile))
    def _():
        cast_rows(lambda rows: w_ref[rows, :])

    @pl.when((pl.program_id(1) == 0) & (j >= first_shifted_tile))
    def _():
        cast_rows(lambda rows: jnp.concatenate([w_ref[rows, :][:, shift:], wn_ref[rows, :][:, :shift]], axis=1))

    o_ref[...] = _mm(a_ref[...], wb_ref[...])


def _proj0(hn, w_in, tm=1088, tn=512):
    m, dm = hn.shape
    shift = AB_STATE - OFF_BETA
    return pl.pallas_call(
        functools.partial(_proj0_kernel, first_shifted_tile=OFF_BETA // tn, shift=shift),
        out_shape=jax.ShapeDtypeStruct((m, P_WIDTH), F32),
        grid=(P_WIDTH // tn, m // tm),
        in_specs=[
            pl.BlockSpec((tm, dm), lambda j, i: (i, 0)),
            pl.BlockSpec((dm, tn), lambda j, i: (0, j)),
            pl.BlockSpec((dm, LANES), lambda j, i: (0, (j + 1) * (tn // LANES))),
        ],
        out_specs=pl.BlockSpec((tm, tn), lambda j, i: (i, j)),
        scratch_shapes=[pltpu.VMEM((dm, tn), BF16)],
        compiler_params=_params("arbitrary", "arbitrary"),
        name="proj0",
    )(hn, w_in, w_in)


def _rows_matmul_kernel(a_ref, w_ref, o_ref):
    o_ref[...] = _mm(a_ref[...], w_ref[...])


def _rows_matmul(a, w, tm=512):
    m, dm = a.shape
    n = w.shape[1]
    return pl.pallas_call(
        _rows_matmul_kernel,
        out_shape=jax.ShapeDtypeStruct((m, n), F32),
        grid=(m // tm,),
        in_specs=[pl.BlockSpec((tm, dm), lambda i: (i, 0)), pl.BlockSpec((dm, n), lambda i: (0, 0))],
        out_specs=pl.BlockSpec((tm, n), lambda i: (i, 0)),
        compiler_params=_params("arbitrary"),
        name="gate_proj",
    )(a, w)


def _split3(a):
    a1 = a.astype(BF16)
    r1 = a - a1.astype(F32)
    a2 = r1.astype(BF16)
    a3 = (r1 - a2.astype(F32)).astype(BF16)
    return a1, a2, a3


def _gates_kernel(ba_ref, perm_ref, alog_ref, dt_ref, o_ref):
    tm = ba_ref.shape[0]
    perm = perm_ref[...]
    r = lax.broadcasted_iota(jnp.int32, (DN_CHUNK, DN_CHUNK), 0)
    c = lax.broadcasted_iota(jnp.int32, (DN_CHUNK, DN_CHUNK), 1)
    lower = (r >= c).astype(BF16)
    upper = (r <= c).astype(BF16)
    ones = jnp.ones((DN_CHUNK, DN_CHUNK), BF16)
    slot = lax.broadcasted_iota(jnp.int32, (DN_CHUNK, LANES), 1) % GATE_SLOTS
    neg_a = -jnp.exp(alog_ref[...])
    dt = dt_ref[...]

    def body(i, _):
        rows = pl.ds(pl.multiple_of(i * DN_CHUNK, DN_CHUNK), DN_CHUNK)
        x1, x2, x3 = _split3(ba_ref[rows, :])
        x = _mm(x1, perm) + _mm(x2, perm) + _mm(x3, perm)
        beta = jax.nn.sigmoid(x)
        g = neg_a * _softplus(x + dt)
        g1, g2, g3 = _split3(g)
        cum = _mm(lower, g1) + _mm(lower, g2) + _mm(lower, g3)
        suf = _mm(upper, g1) + _mm(upper, g2) + _mm(upper, g3)
        tot = _mm(ones, g1) + _mm(ones, g2) + _mm(ones, g3)
        o_ref[rows, :] = jnp.where(slot < 2, beta, jnp.where(slot == 2, cum, jnp.where(slot == 3, suf, tot)))
        return 0

    lax.fori_loop(0, tm // DN_CHUNK, body, 0)


def _gates(ba, alog_l, dt_l, tm=512):
    m = ba.shape[0]
    return pl.pallas_call(
        _gates_kernel,
        out_shape=jax.ShapeDtypeStruct((m, LANES), F32),
        grid=(m // tm,),
        in_specs=[
            pl.BlockSpec((tm, LANES), lambda i: (i, 0)),
            pl.BlockSpec((LANES, LANES), lambda i: (0, 0)),
            pl.BlockSpec((1, LANES), lambda i: (0, 0)),
            pl.BlockSpec((1, LANES), lambda i: (0, 0)),
        ],
        out_specs=pl.BlockSpec((tm, LANES), lambda i: (i, 0)),
        compiler_params=_params("arbitrary"),
        name="dn_gates",
    )(ba, _gate_lane_permutation(), alog_l, dt_l)


HALO = 8


def _fill_halo_scratch(x_ref, xs_ref):
    length, width = x_ref.shape
    zeros = jnp.zeros((HALO, width), F32)
    xs_ref[pl.ds(0, HALO), :] = zeros
    xs_ref[pl.ds(HALO + length, HALO), :] = zeros

    def body(i, _):
        rows = pl.ds(pl.multiple_of(i * DN_CHUNK, DN_CHUNK), DN_CHUNK)
        xs_ref[pl.ds(pl.multiple_of(i * DN_CHUNK + HALO, HALO), DN_CHUNK), :] = x_ref[rows, :]
        return 0

    lax.fori_loop(0, length // DN_CHUNK, body, 0)


def _conv4_chunk(xs_ref, i, w):
    v = xs_ref[pl.ds(pl.multiple_of(i * DN_CHUNK, DN_CHUNK), DN_CHUNK + 2 * HALO), :]
    n = DN_CHUNK
    return (w[0:1] * v[HALO - 2:HALO - 2 + n] + w[1:2] * v[HALO - 1:HALO - 1 + n]
            + w[2:3] * v[HALO:HALO + n] + w[3:4] * v[HALO + 1:HALO + 1 + n])


DN_GROUP = 16
N_TRI_LEVELS = 6
M_STRICT, M_INCL, M_EYE, M_JOIN = 0, 1, 2, 3


def _tri_masks():
    n = DN_CHUNK
    r = np.arange(n)[:, None].repeat(2 * n, axis=1)
    c = np.arange(2 * n)[None, :].repeat(n, axis=0) % n
    upper = np.arange(2 * n)[None, :] >= n
    rr = np.where(upper, c, r)
    cc = np.where(upper, r, c)
    out = [rr > cc, rr >= cc, rr == cc]
    for lvl in range(N_TRI_LEVELS):
        s = 1 << lvl
        out.append((rr // (2 * s) == cc // (2 * s)) & (rr % (2 * s) >= s) & (cc % (2 * s) < s))
    return jnp.asarray(np.stack(out).astype(np.float32))


def _block_diag2(x):
    lane = lax.broadcasted_iota(jnp.int32, x.shape, 1)
    zero = jnp.zeros_like(x)
    return jnp.concatenate([jnp.where(lane < DN_CHUNK, x, zero), jnp.where(lane >= DN_CHUNK, x, zero)], axis=0)


def _packed_tri_inverse(a_list, m_ref):
    d_list = [m_ref[M_EYE] - m_ref[M_JOIN] * a for a in a_list]
    for lvl in range(1, N_TRI_LEVELS):
        join = m_ref[M_JOIN + lvl]
        l_bd = [_block_diag2((join * a).astype(BF16)) for a in a_list]
        d_b = [d.astype(BF16) for d in d_list]
        e_b = [_mm(db, lb).astype(BF16) for db, lb in zip(d_b, l_bd)]
        d_list = [d - _mm(eb, _block_diag2(db)) for d, eb, db in zip(d_list, e_b, d_b)]
    return d_list


def _dn_local_group(h, srcs, g_all, gt_all, slot0, m_ref, nq_ref, c_ref, ge_ref, acc_ref):
    n = DN_CHUNK
    lane = lax.broadcasted_iota(jnp.int32, (n, LANES), 1)
    lane1 = lax.broadcasted_iota(jnp.int32, (1, LANES), 1)
    upper = lane >= n
    strict = m_ref[M_STRICT]
    incl = m_ref[M_INCL] > 0.5
    nt = (((1,), (1,)), ((), ()))
    cols, a_list, decays, kqs = [], [], [], []
    for e, (kc, vc, qc) in enumerate(srcs):
        g = g_all[e * n:(e + 1) * n]

        def col(s, g=g):
            return jnp.sum(jnp.where(lane == h * GATE_SLOTS + s, g, 0.0), axis=1, keepdims=True)

        cs = [col(s) for s in range(6)]
        t0 = (e // 2) * LANES
        row_f = gt_all[2:3, t0:t0 + LANES]
        row_b = gt_all[3:4, t0:t0 + LANES]
        if e % 2 == 0:
            g_row = jnp.where(lane1 < n, row_f, pltpu.roll(row_b, n, axis=1))
        else:
            g_row = jnp.where(lane1 < n, pltpu.roll(row_f, n, axis=1), row_b)
        b_pack = jnp.where(upper, cs[1], cs[0])
        g_pack = jnp.where(upper, cs[3], cs[2])
        decay = jnp.exp(jnp.where(incl, g_pack - g_row, NEG_BIG))
        kb = kc.astype(BF16)
        lhs = kb if qc is None else jnp.concatenate([kb, qc.astype(BF16)], axis=0)
        kq = lax.dot_general(lhs, jnp.concatenate([kb, kb], axis=0), nt, preferred_element_type=F32)
        cols.append(cs)
        decays.append(decay)
        kqs.append(kq)
        a_list.append(strict * (b_pack * kq[:n] * decay))
    t_inv = _packed_tri_inverse(a_list, m_ref)
    for e, (kc, vc, qc) in enumerate(srcs):
        b_f, b_b, g_f, g_b, tot_f, tot_b = cols[e]
        eg = (jnp.exp(g_f), jnp.exp(g_b))
        rhs = jnp.concatenate([jnp.concatenate([kc * (b_f * eg[0]), vc * b_f], axis=1),
                               jnp.concatenate([kc * (b_b * eg[1]), vc * b_b], axis=1)], axis=0)
        wu = _mm(_block_diag2(t_inv[e].astype(BF16)), rhs.astype(BF16))
        slot = slot0 + e
        nq_rows = _mo(slot * 3 * n, n)
        for d in range(2):
            g_col, tot_col = (g_f, tot_f) if d == 0 else (g_b, tot_b)
            ke_t = (kc * jnp.exp(tot_col - g_col)).T.astype(BF16)
            wub = wu[d * n:(d + 1) * n].astype(BF16)
            ge_ref[d, pl.ds(_mo(slot * 8, 8), 8), :] = jnp.broadcast_to(jnp.exp(tot_col[:8]), (8, LANES))
            if qc is None:
                nc = _mm(ke_t, wub)
            else:
                a_qk = (kqs[e][n:] * decays[e])[:, d * n:(d + 1) * n].astype(BF16)
                nc = _mm(jnp.concatenate([ke_t, a_qk], axis=0), wub)
                nq_ref[d, pl.ds(nq_rows + 2 * n, n), :] = (qc * eg[d] - nc[2 * n:, :DN_HEAD_DIM]).astype(BF16)
                rows = pl.ds(_mo((slot - N_CTX_CHUNKS) * n, n), n)
                acc_ref[rows, :] += nc[2 * n:, DN_HEAD_DIM:]
            nq_ref[d, pl.ds(nq_rows, 2 * n), :] = nc[:2 * n, :DN_HEAD_DIM].astype(BF16)
            c_ref[d, pl.ds(_mo(slot * 2 * n, 2 * n), 2 * n), :] = nc[:2 * n, DN_HEAD_DIM:]


def _dn_state_step(d, slot, s, nq_ref, c_ref, ge_ref, with_output):
    n = DN_CHUNK
    nq_rows = _mo(slot * 3 * n, n)
    ge = jnp.tile(ge_ref[d, pl.ds(_mo(slot * 8, 8), 8), :], (DN_HEAD_DIM // 8, 1))
    c = c_ref[d, pl.ds(_mo(slot * 2 * n, 2 * n), 2 * n), :]
    r = _mm(nq_ref[d, pl.ds(nq_rows, (3 if with_output else 2) * n), :], s.astype(BF16))
    return s * ge - r[:2 * n] + c, (r[2 * n:] if with_output else None)


def _conv_silu_chunk(ref, w, start, length, l2_scale):
    n = DN_CHUNK
    lo, hi = (max, min) if isinstance(start, int) else (jnp.maximum, jnp.minimum)
    prev = ref[pl.ds(_mo(lo(start - HALO, 0), HALO), HALO), :]
    nxt = ref[pl.ds(_mo(hi(start + n, length - HALO), HALO), HALO), :]
    prev = jnp.where(start > 0, prev, 0.0)
    nxt = jnp.where(start + n < length, nxt, 0.0)
    v = jnp.concatenate([prev, ref[pl.ds(_mo(start, n), n), :], nxt], axis=0)
    y = (w[0:1] * v[HALO - 2:HALO - 2 + n] + w[1:2] * v[HALO - 1:HALO - 1 + n]
         + w[2:3] * v[HALO:HALO + n] + w[3:4] * v[HALO + 1:HALO + 1 + n])
    a = _silu(y)
    if l2_scale is None:
        return a
    return a * (lax.rsqrt(jnp.sum(a * a, axis=-1, keepdims=True) + EPS) * l2_scale)


def _deltanet_kernel(q_ref, k_ref, v_ref, kc_ref, vc_ref, wq_ref, wk_ref, wv_ref, g_ref, gt_ref, gate_ref, nw_ref,
                     m_ref, o_ref, nq_ref, c_ref, ge_ref, acc_ref):
    h = pl.program_id(1)
    n = DN_CHUNK
    gl = DN_GROUP * n
    acc_ref[...] = jnp.zeros_like(acc_ref)
    wq, wk, wv = wq_ref[...], wk_ref[...], wv_ref[...]

    srcs = [(_conv_silu_chunk(kc_ref, wk, e * n, CTX_LEN, 1.0), _conv_silu_chunk(vc_ref, wv, e * n, CTX_LEN, None), None)
            for e in range(N_CTX_CHUNKS)]
    _dn_local_group(h, srcs, g_ref[pl.ds(0, CTX_LEN), :], gt_ref[:, pl.ds(0, CTX_LEN)], 0,
                    m_ref, nq_ref, c_ref, ge_ref, acc_ref)

    def local_body(p, _):
        base = pl.multiple_of(p * gl, gl)
        gbase = pl.multiple_of(CTX_LEN + p * gl, gl)
        srcs = [(_conv_silu_chunk(k_ref, wk, base + e * n, SEQ, 1.0), _conv_silu_chunk(v_ref, wv, base + e * n, SEQ, None),
                 _conv_silu_chunk(q_ref, wq, base + e * n, SEQ, DN_HEAD_DIM ** -0.5)) for e in range(DN_GROUP)]
        _dn_local_group(h, srcs, g_ref[pl.ds(gbase, gl), :], gt_ref[:, pl.ds(gbase, gl)], N_CTX_CHUNKS + p * DN_GROUP,
                        m_ref, nq_ref, c_ref, ge_ref, acc_ref)
        return 0

    lax.fori_loop(0, N_SEQ_CHUNKS // DN_GROUP, local_body, 0)

    s_f = jnp.zeros((DN_HEAD_DIM, DN_HEAD_DIM), F32)
    s_b = jnp.zeros((DN_HEAD_DIM, DN_HEAD_DIM), F32)
    for i in range(N_CTX_CHUNKS):
        s_f, _ = _dn_state_step(0, i, s_f, nq_ref, c_ref, ge_ref, False)
        s_b, _ = _dn_state_step(1, N_CTX_CHUNKS - 1 - i, s_b, nq_ref, c_ref, ge_ref, False)

    def state_body(i, carry):
        s_f, s_b = carry
        s_f, o_f = _dn_state_step(0, N_CTX_CHUNKS + i, s_f, nq_ref, c_ref, ge_ref, True)
        rows_f = pl.ds(pl.multiple_of(i * n, n), n)
        acc_ref[rows_f, :] += o_f
        jb = N_SEQ_CHUNKS - 1 - i
        s_b, o_b = _dn_state_step(1, N_CTX_CHUNKS + jb, s_b, nq_ref, c_ref, ge_ref, True)
        rows_b = pl.ds(pl.multiple_of(jb * n, n), n)
        acc_ref[rows_b, :] += o_b
        return s_f, s_b

    lax.fori_loop(0, N_SEQ_CHUNKS, state_body, (s_f, s_b))

    nw = nw_ref[...]

    out_rows = 4 * n

    def out_body(i, _):
        rows = pl.ds(pl.multiple_of(i * out_rows, out_rows), out_rows)
        o = acc_ref[rows, :]
        ms = jnp.mean(o * o, axis=-1, keepdims=True)
        o_ref[rows, :] = (o * lax.rsqrt(ms + EPS) * nw * _silu(gate_ref[rows, :])).astype(BF16)
        return 0

    lax.fori_loop(0, SEQ // out_rows, out_body, 0)


def _deltanet(proj, qkv_conv, gates_tok, gates_t, dn_norm):
    n = DN_CHUNK
    hb = DN_WIDTH // DN_HEAD_DIM
    ctx_blk0 = BATCH * SEQ // CTX_LEN
    return pl.pallas_call(
        _deltanet_kernel,
        out_shape=jax.ShapeDtypeStruct((BATCH * SEQ, DN_WIDTH), BF16),
        grid=(BATCH, DN_HEADS),
        in_specs=[
            pl.BlockSpec((SEQ, DN_HEAD_DIM), lambda b, h: (b, h)),
            pl.BlockSpec((SEQ, DN_HEAD_DIM), lambda b, h: (b, hb + h)),
            pl.BlockSpec((SEQ, DN_HEAD_DIM), lambda b, h: (b, 2 * hb + h)),
            pl.BlockSpec((CTX_LEN, DN_HEAD_DIM), lambda b, h: (ctx_blk0 + b, hb + h)),
            pl.BlockSpec((CTX_LEN, DN_HEAD_DIM), lambda b, h: (ctx_blk0 + b, 2 * hb + h)),
            pl.BlockSpec((DN_CONV, DN_HEAD_DIM), lambda b, h: (0, h)),
            pl.BlockSpec((DN_CONV, DN_HEAD_DIM), lambda b, h: (0, hb + h)),
            pl.BlockSpec((DN_CONV, DN_HEAD_DIM), lambda b, h: (0, 2 * hb + h)),
            pl.BlockSpec((None, ALL_LEN, LANES), lambda b, h: (b, 0, 0)),
            pl.BlockSpec((None, GATE_SLOTS, ALL_LEN), lambda b, h: (b, h, 0)),
            pl.BlockSpec((SEQ, DN_HEAD_DIM), lambda b, h: (b, P_DNG // DN_HEAD_DIM + h)),
            pl.BlockSpec((1, DN_HEAD_DIM), lambda b, h: (0, 0)),
            pl.BlockSpec((M_JOIN + N_TRI_LEVELS, n, 2 * n), lambda b, h: (0, 0, 0)),
        ],
        out_specs=pl.BlockSpec((SEQ, DN_HEAD_DIM), lambda b, h: (b, h)),
        scratch_shapes=[
            pltpu.VMEM((2, N_SLOTS * 3 * n, DN_HEAD_DIM), BF16),
            pltpu.VMEM((2, N_SLOTS * 2 * n, DN_HEAD_DIM), F32),
            pltpu.VMEM((2, N_SLOTS * 8, LANES), F32),
            pltpu.VMEM((SEQ, DN_HEAD_DIM), F32),
        ],
        compiler_params=_params("arbitrary", "arbitrary"),
        name="deltanet",
    )(proj, proj, proj, proj, proj, qkv_conv, qkv_conv, qkv_conv, gates_tok, gates_t, proj, dn_norm, _tri_masks())


def _shift_down(y):
    row = lax.broadcasted_iota(jnp.int32, y.shape, 0)
    return jnp.where(row == 0, 0.0, pltpu.roll(y, 1, axis=0))


def _shift_up(y):
    row = lax.broadcasted_iota(jnp.int32, y.shape, 0)
    return jnp.where(row == y.shape[0] - 1, 0.0, pltpu.roll(y, y.shape[0] - 1, axis=0))


def _lru_gate_rows(xc, wcat, bcat, sp, a_ref, b_ref, rows):
    g = _mm(xc.astype(BF16), wcat) + bcat
    w = LRU_BLOCK_DIM
    for d in range(2):
        r = _sigmoid_tanh(g[:, (2 * d) * w:(2 * d + 1) * w])
        i = _sigmoid_tanh(g[:, (2 * d + 1) * w:(2 * d + 2) * w])
        a = jnp.exp(-LRU_C * r * sp[d:d + 1, :])
        a_ref[d, rows, :] = a
        b_ref[d, rows, :] = jnp.sqrt(1.0 - a * a) * (i * xc)


def _seg_rows(j, steps, stride_rows, colmajor):
    if colmajor:
        start = (j % GRID_ROWS) * GRID_W + j // GRID_ROWS
        return pl.ds(start, LRU_SEGS, stride=GRID_W // LRU_SEGS)
    return pl.ds(j, LRU_SEGS, stride=steps)


def _carry_chain(h_end, p_end, c0, reverse):
    row = lax.broadcasted_iota(jnp.int32, h_end.shape, 0)
    carries = jnp.zeros_like(h_end)
    c = c0
    order = range(LRU_SEGS - 1, -1, -1) if reverse else range(LRU_SEGS)
    for gseg in order:
        carries = jnp.where(row == gseg, c, carries)
        c = h_end[gseg:gseg + 1, :] + p_end[gseg:gseg + 1, :] * c
    return carries, c


def _lru_kernel(x_ref, xc_in_ref, gate_ref, cw_ref, cb_ref, wcat_ref, bcat_ref, lam_ref, o_ref,
                xc_ref, xs_ref, xcc_ref, a_ref, b_ref, ac_ref, bc_ref, hl_ref, pp_ref, hs_ref):
    cw = cw_ref[...]
    cb = cb_ref[...]
    wcat = wcat_ref[...]
    bcat = bcat_ref[...]
    sp = _softplus(-lam_ref[...])
    gw = GRID_W

    _fill_halo_scratch(xc_in_ref, xs_ref)
    for i in range(CTX_LEN // DN_CHUNK):
        rows = pl.ds(i * DN_CHUNK, DN_CHUNK)
        xcc_ref[rows, :] = _conv4_chunk(xs_ref, i, cw) + cb
    _lru_gate_rows(xcc_ref[...], wcat, bcat, sp, ac_ref, bc_ref, pl.ds(0, CTX_LEN))

    csteps = CTX_LEN // LRU_SEGS
    zero = jnp.zeros((LRU_SEGS, LRU_BLOCK_DIM), F32)
    one = jnp.ones((LRU_SEGS, LRU_BLOCK_DIM), F32)

    def ctx_body(j, carry):
        hf, pf, hb, pb = carry
        rf = _seg_rows(j, csteps, None, False)
        a = ac_ref[0, rf, :]
        hf = a * hf + bc_ref[0, rf, :]
        pf = a * pf
        rb = _seg_rows(csteps - 1 - j, csteps, None, False)
        a = ac_ref[1, rb, :]
        hb = a * hb + bc_ref[1, rb, :]
        pb = a * pb
        return hf, pf, hb, pb

    hf, pf, hb, pb = lax.fori_loop(0, csteps, ctx_body, (zero, one, zero, one))
    zrow = jnp.zeros((1, LRU_BLOCK_DIM), F32)
    _, h0_f = _carry_chain(hf, pf, zrow, False)
    _, h0_b = _carry_chain(hb, pb, zrow, True)

    def slab(r):
        return x_ref[pl.ds(pl.multiple_of(r * gw, gw), gw), :]

    def conv_slab(r, xm2, xm1, x0, xp1):
        xc_ref[pl.ds(pl.multiple_of(r * gw, gw), gw), :] = (
            cb + cw[0:1] * xm2 + cw[1:2] * xm1 + cw[2:3] * x0 + cw[3:4] * xp1)

    last = GRID_ROWS - 1
    conv_slab(0, _shift_down(slab(last - 1)), _shift_down(slab(last)), slab(0), slab(1))
    conv_slab(1, _shift_down(slab(last)), slab(0), slab(1), slab(2))
    conv_slab(last, slab(last - 2), slab(last - 1), slab(last), _shift_up(slab(0)))

    def conv_body(r, _):
        conv_slab(r, slab(r - 2), slab(r - 1), slab(r), slab(r + 1))
        return 0

    lax.fori_loop(2, last, conv_body, 0)

    gate_rows = 256

    def gate_body(i, _):
        rows = pl.ds(pl.multiple_of(i * gate_rows, gate_rows), gate_rows)
        _lru_gate_rows(xc_ref[rows, :], wcat, bcat, sp, a_ref, b_ref, rows)
        return 0

    lax.fori_loop(0, SEQ // gate_rows, gate_body, 0)

    steps = SEQ // LRU_SEGS

    seg_stride = GRID_W // LRU_SEGS
    unroll = 8

    def seg_rows(wq, r):
        return pl.ds(r * GRID_W + wq, LRU_SEGS, stride=seg_stride)

    def scan_col(wq, carry):
        def scan_rows(rr, carry):
            hf, pf, hb, pb = carry
            for k in range(unroll):
                r = rr * unroll + k
                rf = seg_rows(wq, r)
                a = a_ref[0, rf, :]
                hf = a * hf + b_ref[0, rf, :]
                pf = a * pf
                hl_ref[0, wq * GRID_ROWS + r] = hf
                pp_ref[0, wq * GRID_ROWS + r] = pf
                wqb = seg_stride - 1 - wq
                rb = GRID_ROWS - 1 - r
                rbk = seg_rows(wqb, rb)
                a = a_ref[1, rbk, :]
                hb = a * hb + b_ref[1, rbk, :]
                pb = a * pb
                hl_ref[1, wqb * GRID_ROWS + rb] = hb
                pp_ref[1, wqb * GRID_ROWS + rb] = pb
            return hf, pf, hb, pb

        return lax.fori_loop(0, GRID_ROWS // unroll, scan_rows, carry)

    hf, pf, hb, pb = lax.fori_loop(0, seg_stride, scan_col, (zero, one, zero, one))
    c_f, _ = _carry_chain(hf, pf, h0_f, False)
    c_b, _ = _carry_chain(hb, pb, h0_b, True)

    def fix_col(wq, _):
        def fix_rows(rr, _):
            for k in range(unroll):
                r = rr * unroll + k
                j = wq * GRID_ROWS + r
                h = hl_ref[0, j] + pp_ref[0, j] * c_f + hl_ref[1, j] + pp_ref[1, j] * c_b
                hs_ref[seg_rows(wq, r), :] = h
            return 0

        return lax.fori_loop(0, GRID_ROWS // unroll, fix_rows, 0)

    lax.fori_loop(0, seg_stride, fix_col, 0)

    def out_body(i, _):
        rows = pl.ds(pl.multiple_of(i * gate_rows, gate_rows), gate_rows)
        o_ref[rows, :] = (hs_ref[rows, :] * _silu(gate_ref[rows, :])).astype(BF16)
        return 0

    lax.fori_loop(0, SEQ // gate_rows, out_body, 0)


def _lru(proj, conv_w, conv_b, wcat, bcat, lam):
    w = LRU_BLOCK_DIM
    steps = SEQ // LRU_SEGS
    ctx_blk0 = BATCH * SEQ // CTX_LEN
    return pl.pallas_call(
        _lru_kernel,
        out_shape=jax.ShapeDtypeStruct((BATCH * SEQ, LRU_WIDTH), BF16),
        grid=(BATCH, LRU_BLOCKS),
        in_specs=[
            pl.BlockSpec((SEQ, w), lambda b, n: (b, P_LRU // w + n)),
            pl.BlockSpec((CTX_LEN, w), lambda b, n: (ctx_blk0 + b, P_LRU // w + n)),
            pl.BlockSpec((SEQ, w), lambda b, n: (b, P_LRG // w + n)),
            pl.BlockSpec((LRU_CONV, w), lambda b, n: (0, n)),
            pl.BlockSpec((1, w), lambda b, n: (0, n)),
            pl.BlockSpec((None, w, 4 * w), lambda b, n: (n, 0, 0)),
            pl.BlockSpec((None, 1, 4 * w), lambda b, n: (n, 0, 0)),
            pl.BlockSpec((None, 2, w), lambda b, n: (n, 0, 0)),
        ],
        out_specs=pl.BlockSpec((SEQ, w), lambda b, n: (b, n)),
        scratch_shapes=[
            pltpu.VMEM((SEQ, w), F32),
            pltpu.VMEM((CTX_LEN + 2 * HALO, w), F32),
            pltpu.VMEM((CTX_LEN, w), F32),
            pltpu.VMEM((2, SEQ, w), F32),
            pltpu.VMEM((2, SEQ, w), F32),
            pltpu.VMEM((2, CTX_LEN, w), F32),
            pltpu.VMEM((2, CTX_LEN, w), F32),
            pltpu.VMEM((2, steps, LRU_SEGS, w), F32),
            pltpu.VMEM((2, steps, LRU_SEGS, w), F32),
            pltpu.VMEM((SEQ, w), F32),
        ],
        compiler_params=_params("arbitrary", "arbitrary"),
        name="rglru",
    )(proj, proj, proj, conv_w, conv_b, wcat, bcat, lam)


def _out_proj_kernel(ya_ref, yb_ref, wa_ref, wb_ref, x_ref, g_ref, o_ref, wcast_ref):
    @pl.when(pl.program_id(1) == 0)
    def _():
        def cast_body(r, _):
            rows = pl.ds(pl.multiple_of(r * W_CAST_ROWS, W_CAST_ROWS), W_CAST_ROWS)
            wcast_ref[0, rows, :] = wa_ref[rows, :].astype(BF16)
            wcast_ref[1, rows, :] = wb_ref[rows, :].astype(BF16)
            return 0

        lax.fori_loop(0, wa_ref.shape[0] // W_CAST_ROWS, cast_body, 0)

    acc = _mm(ya_ref[...], wcast_ref[0]) + _mm(yb_ref[...], wcast_ref[1])
    o_ref[...] = x_ref[...] + g_ref[...] * acc


def _out_proj(ya, yb, ya_blk, yb_blk, w, x2d, gate3, tm=1024, tn=512):
    m, dm = x2d.shape
    kh = w.shape[0] // 2
    per_batch = SEQ // tm
    return pl.pallas_call(
        _out_proj_kernel,
        out_shape=jax.ShapeDtypeStruct((m, dm), F32),
        grid=(dm // tn, m // tm),
        in_specs=[
            pl.BlockSpec((tm, kh), lambda j, i: (i, ya_blk)),
            pl.BlockSpec((tm, kh), lambda j, i: (i, yb_blk)),
            pl.BlockSpec((kh, tn), lambda j, i: (0, j)),
            pl.BlockSpec((kh, tn), lambda j, i: (1, j)),
            pl.BlockSpec((tm, tn), lambda j, i: (i, j)),
            pl.BlockSpec((None, 1, tn), lambda j, i: (i // per_batch, 0, j)),
        ],
        out_specs=pl.BlockSpec((tm, tn), lambda j, i: (i, j)),
        scratch_shapes=[pltpu.VMEM((2, kh, tn), BF16)],
        compiler_params=_params("arbitrary", "arbitrary"),
        name="out_proj",
    )(ya, yb, w, w, x2d, gate3)


def _sc_proj_kernel(a_ref, wb_ref, wc_ref, wx_ref, wg_ref, cw_ref, o_ref, wcast_ref, z_ref, bg_ref):
    @pl.when(pl.program_id(1) == 0)
    def _():
        def cast_body(r, _):
            rows = pl.ds(pl.multiple_of(r * W_CAST_ROWS, W_CAST_ROWS), W_CAST_ROWS)
            for gi, w_ref in enumerate((wb_ref, wc_ref, wx_ref, wg_ref)):
                wcast_ref[gi, rows, :] = w_ref[rows, :].astype(BF16)
            return 0

        lax.fori_loop(0, wb_ref.shape[0] // W_CAST_ROWS, cast_body, 0)

    hn = a_ref[...]
    z_ref[...] = _mm(hn, wcast_ref[1]) * _mm(hn, wcast_ref[2])
    bg_ref[...] = _mm(hn, wcast_ref[0]) * _silu(_mm(hn, wcast_ref[3]))
    cw = cw_ref[...]
    gw = GRID_W

    def body(r, _):
        rows = pl.ds(pl.multiple_of(r * gw, gw), gw)
        z = z_ref[rows, :]
        zc = cw[0:1] * _shift_down(z) + cw[1:2] * z + cw[2:3] * _shift_up(z)
        o_ref[rows, :] = (bg_ref[rows, :] * zc).astype(BF16)
        return 0

    lax.fori_loop(0, a_ref.shape[0] // gw, body, 0)


def _sc_proj(hn, w, conv_w, tm=512, tc=256):
    m, dm = hn.shape
    nt = SC_WIDTH // tc
    return pl.pallas_call(
        _sc_proj_kernel,
        out_shape=jax.ShapeDtypeStruct((m, SC_WIDTH), BF16),
        grid=(nt, m // tm),
        in_specs=[
            pl.BlockSpec((tm, dm), lambda j, i: (i, 0)),
            pl.BlockSpec((dm, tc), lambda j, i: (0, j)),
            pl.BlockSpec((dm, tc), lambda j, i: (0, nt + j)),
            pl.BlockSpec((dm, tc), lambda j, i: (0, 2 * nt + j)),
            pl.BlockSpec((dm, tc), lambda j, i: (0, 3 * nt + j)),
            pl.BlockSpec((SC_CONV, tc), lambda j, i: (0, j)),
        ],
        out_specs=pl.BlockSpec((tm, tc), lambda j, i: (i, j)),
        scratch_shapes=[pltpu.VMEM((4, dm, tc), BF16), pltpu.VMEM((tm, tc), F32), pltpu.VMEM((tm, tc), F32)],
        compiler_params=_params("arbitrary", "arbitrary"),
        name="sc_proj",
    )(hn, w, w, w, w, conv_w)


def _final_norm_kernel(x_ref, w_ref, o_ref):
    w = w_ref[...]

    def body(r, _):
        rows = pl.ds(pl.multiple_of(r * 16, 16), 16)
        x = x_ref[rows, :]
        ms = jnp.mean(x * x, axis=-1, keepdims=True)
        o_ref[rows, :] = x * lax.rsqrt(ms + EPS) * w
        return 0

    lax.fori_loop(0, x_ref.shape[0] // 16, body, 0)


def _final_norm(x2d, w, tm=256):
    m, dm = x2d.shape
    return pl.pallas_call(
        _final_norm_kernel,
        out_shape=jax.ShapeDtypeStruct((m, dm), F32),
        grid=(m // tm,),
        in_specs=[pl.BlockSpec((tm, dm), lambda i: (i, 0)), pl.BlockSpec((1, dm), lambda i: (0, 0))],
        out_specs=pl.BlockSpec((tm, dm), lambda i: (i, 0)),
        compiler_params=_params("arbitrary"),
        name="final_norm",
    )(x2d, w)


def _gate_lane_vector(p):
    z = jnp.zeros((DN_HEADS,), F32)
    cols = jnp.stack([z, z, p[0], p[1], p[0], p[1], z, z], axis=1)
    return cols.reshape(1, LANES).astype(F32)


def _gate_weight(w_in):
    return w_in[:, OFF_BETA:OFF_BETA + LANES].astype(BF16)


def _gate_lane_permutation():
    p = np.zeros((LANES, LANES), np.float32)
    for h in range(DN_HEADS):
        for slot, src in enumerate([h, DN_HEADS + h, 2 * DN_HEADS + h, 3 * DN_HEADS + h, 2 * DN_HEADS + h, 3 * DN_HEADS + h]):
            p[src, h * GATE_SLOTS + slot] = 1.0
    return jnp.asarray(p, BF16)


def kernel(x, c, ctx, c_ctx, mod_w, mod_b, norm_w, ab_w_in, ab_qkv_conv, ab_a_log, ab_dt_bias, ab_dn_norm,
           ab_lru_conv_w, ab_lru_conv_b, ab_lru_w_r, ab_lru_b_r, ab_lru_w_i, ab_lru_b_i, ab_lru_lambda, ab_w_out,
           sc_w_in, sc_conv, sc_w_out, final_norm_w):
    dm = D_MODEL
    x2d = x.reshape(BATCH * SEQ, dm)
    ctx2d = ctx.reshape(BATCH * CTX_LEN, dm)

    c8 = jnp.concatenate([c, c_ctx[None, :], jnp.zeros((8 - BATCH - 1, dm), F32)], axis=0)
    mod = _modulation(c8, mod_w, mod_b)
    shift = [mod[l, :, :dm].reshape(8, 1, dm) for l in range(2)]
    scale = [mod[l, :, dm:2 * dm].reshape(8, 1, dm) for l in range(2)]
    gate = [mod[l, :, 2 * dm:].reshape(8, 1, dm) for l in range(2)]

    w_in = ab_w_in[0]
    nw0 = norm_w[0].reshape(1, dm)
    hn = _norm_mod(x2d, ctx2d, nw0, scale[0], shift[0])
    proj = _proj0(hn, w_in)
    ba = _rows_matmul(hn, _gate_weight(w_in))

    n_lat = BATCH * SEQ
    ba_all = jnp.concatenate([ba[n_lat:].reshape(BATCH, CTX_LEN, LANES), ba[:n_lat].reshape(BATCH, SEQ, LANES)], axis=1)
    gates_tok = _gates(ba_all.reshape(BATCH * ALL_LEN, LANES), _gate_lane_vector(ab_a_log[0]),
                       _gate_lane_vector(ab_dt_bias[0])).reshape(BATCH, ALL_LEN, LANES)
    gates_t = gates_tok.transpose(0, 2, 1)

    y_dn = _deltanet(proj, ab_qkv_conv[0], gates_tok, gates_t, ab_dn_norm[0].reshape(1, DN_HEAD_DIM))

    wcat = jnp.concatenate([ab_lru_w_r[0, 0], ab_lru_w_i[0, 0], ab_lru_w_r[0, 1], ab_lru_w_i[0, 1]],
                           axis=-1).astype(BF16)
    bl = LRU_BLOCK_DIM
    bcat = jnp.concatenate([ab_lru_b_r[0, 0].reshape(LRU_BLOCKS, 1, bl), ab_lru_b_i[0, 0].reshape(LRU_BLOCKS, 1, bl),
                            ab_lru_b_r[0, 1].reshape(LRU_BLOCKS, 1, bl), ab_lru_b_i[0, 1].reshape(LRU_BLOCKS, 1, bl)],
                           axis=-1)
    lam = ab_lru_lambda[0].reshape(2, LRU_BLOCKS, bl).transpose(1, 0, 2)
    y_lru = _lru(proj, ab_lru_conv_w[0], ab_lru_conv_b[0].reshape(1, LRU_WIDTH), wcat, bcat, lam)

    x1 = _out_proj(y_dn, y_lru, 0, 0, ab_w_out[0], x2d, gate[0])

    hn1 = _norm_mod(x1, None, norm_w[1].reshape(1, dm), scale[1], shift[1])
    y1 = _sc_proj(hn1, sc_w_in[0], sc_conv[0])
    x2 = _out_proj(y1, y1, 0, 1, sc_w_out[0], x1, gate[1])

    return _final_norm(x2, final_norm_w.reshape(1, dm)).reshape(BATCH, SEQ, dm)
```

```python
import functools

import jax
import jax.numpy as jnp
import numpy as np
from jax import lax
from jax.experimental import pallas as pl
from jax.experimental.pallas import tpu as pltpu

F32 = jnp.float32
BF16 = jnp.bfloat16

D_MODEL = 4096
BATCH = 2
SEQ = 4096
CTX_LEN = 256
GRID_W = 64
GRID_ROWS = SEQ // GRID_W
EPS = 1e-6

DN_HEADS = 16
DN_HEAD_DIM = 128
DN_WIDTH = DN_HEADS * DN_HEAD_DIM
DN_CONV = 4
DN_CHUNK = 64

LRU_WIDTH = 2048
LRU_BLOCKS = 16
LRU_BLOCK_DIM = LRU_WIDTH // LRU_BLOCKS
LRU_CONV = 4
LRU_C = 8.0

SC_WIDTH = D_MODEL
SC_CONV = 3

OFF_LRU = 3 * DN_WIDTH
OFF_BETA = OFF_LRU + LRU_WIDTH
OFF_ALPHA = OFF_BETA + 2 * DN_HEADS
AB_STATE = OFF_ALPHA + 2 * DN_HEADS

P_K = DN_WIDTH
P_V = 2 * DN_WIDTH
P_LRU = 3 * DN_WIDTH
P_DNG = P_LRU + LRU_WIDTH
P_LRG = P_DNG + DN_WIDTH
P_WIDTH = P_LRG + LRU_WIDTH
LANES = 128
GATE_SLOTS = LANES // DN_HEADS

N_CTX_CHUNKS = CTX_LEN // DN_CHUNK
N_SEQ_CHUNKS = SEQ // DN_CHUNK
N_SLOTS = N_CTX_CHUNKS + N_SEQ_CHUNKS
ALL_LEN = CTX_LEN + SEQ

LRU_SEGS = 8
NEG_BIG = -1e30

VMEM_LIMIT = 56 * 1024 * 1024


def _params(*sem):
    return pltpu.CompilerParams(dimension_semantics=sem, vmem_limit_bytes=VMEM_LIMIT)


def _mm(a, b):
    return jnp.dot(a, b, preferred_element_type=F32)


def _mo(v, m):
    return v if isinstance(v, int) else pl.multiple_of(v, m)


def _silu(x):
    return x * jax.nn.sigmoid(x)


def _sigmoid_tanh(x):
    return 0.5 * jnp.tanh(0.5 * x) + 0.5


def _softplus(x):
    return jnp.maximum(x, 0.0) + jnp.log1p(jnp.exp(-jnp.abs(x)))


def _mod_kernel(c_ref, w_ref, b_ref, o_ref):
    s = _silu(c_ref[...]).astype(BF16)
    o_ref[...] = _mm(s, w_ref[...].astype(BF16)) + b_ref[...]


def _modulation(c8, mod_w, mod_b):
    depth, dm, n = mod_w.shape
    tn = 512
    return pl.pallas_call(
        _mod_kernel,
        out_shape=jax.ShapeDtypeStruct((depth, 8, n), F32),
        grid=(depth, n // tn),
        in_specs=[
            pl.BlockSpec((8, dm), lambda l, j: (0, 0)),
            pl.BlockSpec((None, dm, tn), lambda l, j: (l, 0, j)),
            pl.BlockSpec((None, 1, tn), lambda l, j: (l, 0, j)),
        ],
        out_specs=pl.BlockSpec((None, 8, tn), lambda l, j: (l, 0, j)),
        compiler_params=_params("arbitrary", "arbitrary"),
        name="modulation",
    )(c8, mod_w, mod_b.reshape(depth, 1, n))


NORM_ROWS = 8


def _norm_rows(x_ref, nw_ref, sc_ref, sh_ref, hn_ref):
    tm = x_ref.shape[0]
    wsc = nw_ref[...] * (1.0 + sc_ref[...])
    sh = sh_ref[...]

    def body(r, _):
        rows = pl.ds(pl.multiple_of(r * NORM_ROWS, NORM_ROWS), NORM_ROWS)
        x = x_ref[rows, :]
        ms = jnp.mean(x * x, axis=-1, keepdims=True)
        hn_ref[rows, :] = (x * lax.rsqrt(ms + EPS) * wsc + sh).astype(BF16)
        return 0

    lax.fori_loop(0, tm // NORM_ROWS, body, 0, unroll=4)


def _norm_mod_kernel(x_ref, c_ref, nw_ref, sc_ref, sh_ref, o_ref, *, n_x_tiles):
    @pl.when(pl.program_id(0) < n_x_tiles)
    def _():
        _norm_rows(x_ref, nw_ref, sc_ref, sh_ref, o_ref)

    @pl.when(pl.program_id(0) >= n_x_tiles)
    def _():
        _norm_rows(c_ref, nw_ref, sc_ref, sh_ref, o_ref)


def _norm_mod(x2d, c2d, nw, sc3, sh3, tm=512):
    m, dm = x2d.shape
    mc = 0 if c2d is None else c2d.shape[0]
    c2d = x2d if c2d is None else c2d
    n_x_tiles = m // tm
    per_batch = SEQ // tm
    mod_spec = pl.BlockSpec((None, 1, dm), lambda i: (i // per_batch, 0, 0))
    return pl.pallas_call(
        functools.partial(_norm_mod_kernel, n_x_tiles=n_x_tiles),
        out_shape=jax.ShapeDtypeStruct((m + mc, dm), BF16),
        grid=(n_x_tiles + mc // tm,),
        in_specs=[
            pl.BlockSpec((tm, dm), lambda i: (jnp.minimum(i, n_x_tiles - 1), 0)),
            pl.BlockSpec((tm, dm), lambda i: (jnp.maximum(i - n_x_tiles, 0), 0)),
            pl.BlockSpec((1, dm), lambda i: (0, 0)),
            mod_spec,
            mod_spec,
        ],
        out_specs=pl.BlockSpec((tm, dm), lambda i: (i, 0)),
        compiler_params=_params("arbitrary"),
        name="norm_mod",
    )(x2d, c2d, nw, sc3, sh3)


W_CAST_ROWS = 64


def _proj0_kernel(a_ref, w_ref, wn_ref, o_ref, wb_ref, *, first_shifted_tile, shift):
    j = pl.program_id(0)
    tn, dm = w_ref.shape

    def cast_slabs(fn):
        def body(c, _):
            cols = pl.ds(pl.multiple_of(c * LANES, LANES), LANES)
            wb_ref[cols, :] = fn(cols).T.astype(BF16)
            return 0

        lax.fori_loop(0, dm // LANES, body, 0)

    @pl.when((pl.program_id(1) == 0) & (j < first_shifted_tile))
    def _():
        cast_slabs(lambda cols: w_ref[:, cols])

    @pl.when((pl.program_id(1) == 0) & (j >= first_shifted_tile))
    def _():
        cast_slabs(lambda cols: jnp.concatenate([w_ref[pl.ds(shift, tn - shift), cols], wn_ref[pl.ds(0, shift), cols]],
                                                axis=0))

    o_ref[...] = _mm(a_ref[...], wb_ref[...])


def _proj0(hn, w_in_t, tm=1088, tn=512):
    m, dm = hn.shape
    shift = AB_STATE - OFF_BETA
    return pl.pallas_call(
        functools.partial(_proj0_kernel, first_shifted_tile=OFF_BETA // tn, shift=shift),
        out_shape=jax.ShapeDtypeStruct((m, P_WIDTH), F32),
        grid=(P_WIDTH // tn, m // tm),
        in_specs=[
            pl.BlockSpec((tm, dm), lambda j, i: (i, 0)),
            pl.BlockSpec((tn, dm), lambda j, i: (j, 0)),
            pl.BlockSpec((LANES, dm), lambda j, i: ((j + 1) * (tn // LANES), 0)),
        ],
        out_specs=pl.BlockSpec((tm, tn), lambda j, i: (i, j)),
        scratch_shapes=[pltpu.VMEM((dm, tn), BF16)],
        compiler_params=_params("arbitrary", "arbitrary"),
        name="proj0",
    )(hn, w_in_t, w_in_t)


def _gate_proj_kernel(a_ref, w_ref, o_ref, wb_ref):
    @pl.when(pl.program_id(0) == 0)
    def _():
        wb_ref[...] = w_ref[...].astype(BF16)

    o_ref[...] = lax.dot_general(a_ref[...], wb_ref[...], (((1,), (1,)), ((), ())), preferred_element_type=F32)


def _gate_proj(a, w_in_t, tm=512):
    m, dm = a.shape
    return pl.pallas_call(
        _gate_proj_kernel,
        out_shape=jax.ShapeDtypeStruct((m, LANES), F32),
        grid=(m // tm,),
        in_specs=[pl.BlockSpec((tm, dm), lambda i: (i, 0)), pl.BlockSpec((LANES, dm), lambda i: (OFF_BETA // LANES, 0))],
        out_specs=pl.BlockSpec((tm, LANES), lambda i: (i, 0)),
        scratch_shapes=[pltpu.VMEM((LANES, dm), BF16)],
        compiler_params=_params("arbitrary"),
        name="gate_proj",
    )(a, w_in_t)


def _split3(a):
    a1 = a.astype(BF16)
    r1 = a - a1.astype(F32)
    a2 = r1.astype(BF16)
    a3 = (r1 - a2.astype(F32)).astype(BF16)
    return a1, a2, a3


def _gates_kernel(ba_ref, perm_ref, alog_ref, dt_ref, o_ref):
    tm = ba_ref.shape[0]
    perm = perm_ref[...]
    r = lax.broadcasted_iota(jnp.int32, (DN_CHUNK, DN_CHUNK), 0)
    c = lax.broadcasted_iota(jnp.int32, (DN_CHUNK, DN_CHUNK), 1)
    lower = (r >= c).astype(BF16)
    upper = (r <= c).astype(BF16)
    ones = jnp.ones((DN_CHUNK, DN_CHUNK), BF16)
    slot = lax.broadcasted_iota(jnp.int32, (DN_CHUNK, LANES), 1) % GATE_SLOTS
    neg_a = -jnp.exp(alog_ref[...])
    dt = dt_ref[...]

    def body(i, _):
        rows = pl.ds(pl.multiple_of(i * DN_CHUNK, DN_CHUNK), DN_CHUNK)
        x1, x2, x3 = _split3(ba_ref[rows, :])
        x = _mm(x1, perm) + _mm(x2, perm) + _mm(x3, perm)
        beta = jax.nn.sigmoid(x)
        g = neg_a * _softplus(x + dt)
        g1, g2, g3 = _split3(g)
        cum = _mm(lower, g1) + _mm(lower, g2) + _mm(lower, g3)
        suf = _mm(upper, g1) + _mm(upper, g2) + _mm(upper, g3)
        tot = _mm(ones, g1) + _mm(ones, g2) + _mm(ones, g3)
        o_ref[rows, :] = jnp.where(slot < 2, beta, jnp.where(slot == 2, cum, jnp.where(slot == 3, suf, tot)))
        return 0

    lax.fori_loop(0, tm // DN_CHUNK, body, 0)


def _gates(ba, alog_l, dt_l, tm=512):
    m = ba.shape[0]
    return pl.pallas_call(
        _gates_kernel,
        out_shape=jax.ShapeDtypeStruct((m, LANES), F32),
        grid=(m // tm,),
        in_specs=[
            pl.BlockSpec((tm, LANES), lambda i: (i, 0)),
            pl.BlockSpec((LANES, LANES), lambda i: (0, 0)),
            pl.BlockSpec((1, LANES), lambda i: (0, 0)),
            pl.BlockSpec((1, LANES), lambda i: (0, 0)),
        ],
        out_specs=pl.BlockSpec((tm, LANES), lambda i: (i, 0)),
        compiler_params=_params("arbitrary"),
        name="dn_gates",
    )(ba, _gate_lane_permutation(), alog_l, dt_l)


HALO = 8


def _fill_halo_scratch(x_ref, xs_ref):
    length, width = x_ref.shape
    zeros = jnp.zeros((HALO, width), F32)
    xs_ref[pl.ds(0, HALO), :] = zeros
    xs_ref[pl.ds(HALO + length, HALO), :] = zeros

    def body(i, _):
        rows = pl.ds(pl.multiple_of(i * DN_CHUNK, DN_CHUNK), DN_CHUNK)
        xs_ref[pl.ds(pl.multiple_of(i * DN_CHUNK + HALO, HALO), DN_CHUNK), :] = x_ref[rows, :]
        return 0

    lax.fori_loop(0, length // DN_CHUNK, body, 0)


def _conv4_chunk(xs_ref, i, w):
    v = xs_ref[pl.ds(pl.multiple_of(i * DN_CHUNK, DN_CHUNK), DN_CHUNK + 2 * HALO), :]
    n = DN_CHUNK
    return (w[0:1] * v[HALO - 2:HALO - 2 + n] + w[1:2] * v[HALO - 1:HALO - 1 + n]
            + w[2:3] * v[HALO:HALO + n] + w[3:4] * v[HALO + 1:HALO + 1 + n])


DN_GROUP = 16
N_TRI_LEVELS = 6
M_STRICT, M_INCL, M_EYE, M_JOIN = 0, 1, 2, 3


def _tri_masks():
    n = DN_CHUNK
    r = np.arange(n)[:, None].repeat(2 * n, axis=1)
    c = np.arange(2 * n)[None, :].repeat(n, axis=0) % n
    upper = np.arange(2 * n)[None, :] >= n
    rr = np.where(upper, c, r)
    cc = np.where(upper, r, c)
    out = [rr > cc, rr >= cc, rr == cc]
    for lvl in range(N_TRI_LEVELS):
        s = 1 << lvl
        out.append((rr // (2 * s) == cc // (2 * s)) & (rr % (2 * s) >= s) & (cc % (2 * s) < s))
    return jnp.asarray(np.stack(out).astype(np.float32))


def _block_diag2(x):
    lane = lax.broadcasted_iota(jnp.int32, x.shape, 1)
    zero = jnp.zeros_like(x)
    return jnp.concatenate([jnp.where(lane < DN_CHUNK, x, zero), jnp.where(lane >= DN_CHUNK, x, zero)], axis=0)


def _packed_tri_inverse(a_list, m_ref):
    d_list = [m_ref[M_EYE] - m_ref[M_JOIN] * a for a in a_list]
    for lvl in range(1, N_TRI_LEVELS):
        join = m_ref[M_JOIN + lvl]
        l_bd = [_block_diag2((join * a).astype(BF16)) for a in a_list]
        d_b = [d.astype(BF16) for d in d_list]
        e_b = [_mm(db, lb).astype(BF16) for db, lb in zip(d_b, l_bd)]
        d_list = [d - _mm(eb, _block_diag2(db)) for d, eb, db in zip(d_list, e_b, d_b)]
    return d_list


def _dn_local_group(h, srcs, g_all, gt_all, slot0, m_ref, nq_ref, c_ref, ge_ref, acc_ref):
    n = DN_CHUNK
    lane = lax.broadcasted_iota(jnp.int32, (n, LANES), 1)
    lane1 = lax.broadcasted_iota(jnp.int32, (1, LANES), 1)
    upper = lane >= n
    strict = m_ref[M_STRICT]
    incl = m_ref[M_INCL] > 0.5
    nt = (((1,), (1,)), ((), ()))
    cols, a_list, decays, kqs = [], [], [], []
    for e, (kc, vc, qc) in enumerate(srcs):
        g = g_all[e * n:(e + 1) * n]

        def col(s, g=g):
            return jnp.sum(jnp.where(lane == h * GATE_SLOTS + s, g, 0.0), axis=1, keepdims=True)

        cs = [col(s) for s in range(6)]
        t0 = (e // 2) * LANES
        row_f = gt_all[2:3, t0:t0 + LANES]
        row_b = gt_all[3:4, t0:t0 + LANES]
        if e % 2 == 0:
            g_row = jnp.where(lane1 < n, row_f, pltpu.roll(row_b, n, axis=1))
        else:
            g_row = jnp.where(lane1 < n, pltpu.roll(row_f, n, axis=1), row_b)
        b_pack = jnp.where(upper, cs[1], cs[0])
        g_pack = jnp.where(upper, cs[3], cs[2])
        decay = jnp.exp(jnp.where(incl, g_pack - g_row, NEG_BIG))
        kb = kc.astype(BF16)
        lhs = kb if qc is None else jnp.concatenate([kb, qc.astype(BF16)], axis=0)
        kq = lax.dot_general(lhs, jnp.concatenate([kb, kb], axis=0), nt, preferred_element_type=F32)
        cols.append(cs)
        decays.append(decay)
        kqs.append(kq)
        a_list.append(strict * (b_pack * kq[:n] * decay))
    t_inv = _packed_tri_inverse(a_list, m_ref)
    for e, (kc, vc, qc) in enumerate(srcs):
        b_f, b_b, g_f, g_b, tot_f, tot_b = cols[e]
        eg = (jnp.exp(g_f), jnp.exp(g_b))
        rhs = jnp.concatenate([jnp.concatenate([kc * (b_f * eg[0]), vc * b_f], axis=1),
                               jnp.concatenate([kc * (b_b * eg[1]), vc * b_b], axis=1)], axis=0)
        wu = _mm(_block_diag2(t_inv[e].astype(BF16)), rhs.astype(BF16))
        slot = slot0 + e
        nq_rows = _mo(slot * 3 * n, n)
        for d in range(2):
            g_col, tot_col = (g_f, tot_f) if d == 0 else (g_b, tot_b)
            ke_t = (kc * jnp.exp(tot_col - g_col)).T.astype(BF16)
            wub = wu[d * n:(d + 1) * n].astype(BF16)
            ge_ref[d, pl.ds(_mo(slot * 8, 8), 8), :] = jnp.broadcast_to(jnp.exp(tot_col[:8]), (8, LANES))
            if qc is None:
                nc = _mm(ke_t, wub)
            else:
                a_qk = (kqs[e][n:] * decays[e])[:, d * n:(d + 1) * n].astype(BF16)
                nc = _mm(jnp.concatenate([ke_t, a_qk], axis=0), wub)
                nq_ref[d, pl.ds(nq_rows + 2 * n, n), :] = (qc * eg[d] - nc[2 * n:, :DN_HEAD_DIM]).astype(BF16)
                rows = pl.ds(_mo((slot - N_CTX_CHUNKS) * n, n), n)
                acc_ref[rows, :] += nc[2 * n:, DN_HEAD_DIM:]
            nq_ref[d, pl.ds(nq_rows, 2 * n), :] = nc[:2 * n, :DN_HEAD_DIM].astype(BF16)
            c_ref[d, pl.ds(_mo(slot * 2 * n, 2 * n), 2 * n), :] = nc[:2 * n, DN_HEAD_DIM:]


def _dn_state_step(d, slot, s, nq_ref, c_ref, ge_ref, with_output):
    n = DN_CHUNK
    nq_rows = _mo(slot * 3 * n, n)
    ge = jnp.tile(ge_ref[d, pl.ds(_mo(slot * 8, 8), 8), :], (DN_HEAD_DIM // 8, 1))
    c = c_ref[d, pl.ds(_mo(slot * 2 * n, 2 * n), 2 * n), :]
    r = _mm(nq_ref[d, pl.ds(nq_rows, (3 if with_output else 2) * n), :], s.astype(BF16))
    return s * ge - r[:2 * n] + c, (r[2 * n:] if with_output else None)


def _conv_silu_chunk(ref, w, start, length, l2_scale):
    n = DN_CHUNK
    lo, hi = (max, min) if isinstance(start, int) else (jnp.maximum, jnp.minimum)
    prev = ref[pl.ds(_mo(lo(start - HALO, 0), HALO), HALO), :]
    nxt = ref[pl.ds(_mo(hi(start + n, length - HALO), HALO), HALO), :]
    prev = jnp.where(start > 0, prev, 0.0)
    nxt = jnp.where(start + n < length, nxt, 0.0)
    v = jnp.concatenate([prev, ref[pl.ds(_mo(start, n), n), :], nxt], axis=0)
    y = (w[0:1] * v[HALO - 2:HALO - 2 + n] + w[1:2] * v[HALO - 1:HALO - 1 + n]
         + w[2:3] * v[HALO:HALO + n] + w[3:4] * v[HALO + 1:HALO + 1 + n])
    a = _silu(y)
    if l2_scale is None:
        return a
    return a * (lax.rsqrt(jnp.sum(a * a, axis=-1, keepdims=True) + EPS) * l2_scale)


def _deltanet_kernel(q_ref, k_ref, v_ref, kc_ref, vc_ref, wq_ref, wk_ref, wv_ref, g_ref, gt_ref, gate_ref, nw_ref,
                     m_ref, o_ref, nq_ref, c_ref, ge_ref, acc_ref):
    h = pl.program_id(1)
    n = DN_CHUNK
    gl = DN_GROUP * n
    acc_ref[...] = jnp.zeros_like(acc_ref)
    wq, wk, wv = wq_ref[...], wk_ref[...], wv_ref[...]

    srcs = [(_conv_silu_chunk(kc_ref, wk, e * n, CTX_LEN, 1.0), _conv_silu_chunk(vc_ref, wv, e * n, CTX_LEN, None), None)
            for e in range(N_CTX_CHUNKS)]
    _dn_local_group(h, srcs, g_ref[pl.ds(0, CTX_LEN), :], gt_ref[:, pl.ds(0, CTX_LEN)], 0,
                    m_ref, nq_ref, c_ref, ge_ref, acc_ref)

    def local_body(p, _):
        base = pl.multiple_of(p * gl, gl)
        gbase = pl.multiple_of(CTX_LEN + p * gl, gl)
        srcs = [(_conv_silu_chunk(k_ref, wk, base + e * n, SEQ, 1.0), _conv_silu_chunk(v_ref, wv, base + e * n, SEQ, None),
                 _conv_silu_chunk(q_ref, wq, base + e * n, SEQ, DN_HEAD_DIM ** -0.5)) for e in range(DN_GROUP)]
        _dn_local_group(h, srcs, g_ref[pl.ds(gbase, gl), :], gt_ref[:, pl.ds(gbase, gl)], N_CTX_CHUNKS + p * DN_GROUP,
                        m_ref, nq_ref, c_ref, ge_ref, acc_ref)
        return 0

    lax.fori_loop(0, N_SEQ_CHUNKS // DN_GROUP, local_body, 0)

    s_f = jnp.zeros((DN_HEAD_DIM, DN_HEAD_DIM), F32)
    s_b = jnp.zeros((DN_HEAD_DIM, DN_HEAD_DIM), F32)
    for i in range(N_CTX_CHUNKS):
        s_f, _ = _dn_state_step(0, i, s_f, nq_ref, c_ref, ge_ref, False)
        s_b, _ = _dn_state_step(1, N_CTX_CHUNKS - 1 - i, s_b, nq_ref, c_ref, ge_ref, False)

    def state_body(i, carry):
        s_f, s_b = carry
        s_f, o_f = _dn_state_step(0, N_CTX_CHUNKS + i, s_f, nq_ref, c_ref, ge_ref, True)
        rows_f = pl.ds(pl.multiple_of(i * n, n), n)
        acc_ref[rows_f, :] += o_f
        jb = N_SEQ_CHUNKS - 1 - i
        s_b, o_b = _dn_state_step(1, N_CTX_CHUNKS + jb, s_b, nq_ref, c_ref, ge_ref, True)
        rows_b = pl.ds(pl.multiple_of(jb * n, n), n)
        acc_ref[rows_b, :] += o_b
        return s_f, s_b

    lax.fori_loop(0, N_SEQ_CHUNKS, state_body, (s_f, s_b))

    nw = nw_ref[...]

    out_rows = 4 * n

    def out_body(i, _):
        rows = pl.ds(pl.multiple_of(i * out_rows, out_rows), out_rows)
        o = acc_ref[rows, :]
        ms = jnp.mean(o * o, axis=-1, keepdims=True)
        o_ref[rows, :] = (o * lax.rsqrt(ms + EPS) * nw * _silu(gate_ref[rows, :])).astype(BF16)
        return 0

    lax.fori_loop(0, SEQ // out_rows, out_body, 0)


def _deltanet(proj, qkv_conv, gates_tok, gates_t, dn_norm):
    n = DN_CHUNK
    hb = DN_WIDTH // DN_HEAD_DIM
    ctx_blk0 = BATCH * SEQ // CTX_LEN
    return pl.pallas_call(
        _deltanet_kernel,
        out_shape=jax.ShapeDtypeStruct((BATCH * SEQ, DN_WIDTH), BF16),
        grid=(BATCH, DN_HEADS),
        in_specs=[
            pl.BlockSpec((SEQ, DN_HEAD_DIM), lambda b, h: (b, h)),
            pl.BlockSpec((SEQ, DN_HEAD_DIM), lambda b, h: (b, hb + h)),
            pl.BlockSpec((SEQ, DN_HEAD_DIM), lambda b, h: (b, 2 * hb + h)),
            pl.BlockSpec((CTX_LEN, DN_HEAD_DIM), lambda b, h: (ctx_blk0 + b, hb + h)),
            pl.BlockSpec((CTX_LEN, DN_HEAD_DIM), lambda b, h: (ctx_blk0 + b, 2 * hb + h)),
            pl.BlockSpec((DN_CONV, DN_HEAD_DIM), lambda b, h: (0, h)),
            pl.BlockSpec((DN_CONV, DN_HEAD_DIM), lambda b, h: (0, hb + h)),
            pl.BlockSpec((DN_CONV, DN_HEAD_DIM), lambda b, h: (0, 2 * hb + h)),
            pl.BlockSpec((None, ALL_LEN, LANES), lambda b, h: (b, 0, 0)),
            pl.BlockSpec((None, GATE_SLOTS, ALL_LEN), lambda b, h: (b, h, 0)),
            pl.BlockSpec((SEQ, DN_HEAD_DIM), lambda b, h: (b, P_DNG // DN_HEAD_DIM + h)),
            pl.BlockSpec((1, DN_HEAD_DIM), lambda b, h: (0, 0)),
            pl.BlockSpec((M_JOIN + N_TRI_LEVELS, n, 2 * n), lambda b, h: (0, 0, 0)),
        ],
        out_specs=pl.BlockSpec((SEQ, DN_HEAD_DIM), lambda b, h: (b, h)),
        scratch_shapes=[
            pltpu.VMEM((2, N_SLOTS * 3 * n, DN_HEAD_DIM), BF16),
            pltpu.VMEM((2, N_SLOTS * 2 * n, DN_HEAD_DIM), F32),
            pltpu.VMEM((2, N_SLOTS * 8, LANES), F32),
            pltpu.VMEM((SEQ, DN_HEAD_DIM), F32),
        ],
        compiler_params=_params("arbitrary", "arbitrary"),
        name="deltanet",
    )(proj, proj, proj, proj, proj, qkv_conv, qkv_conv, qkv_conv, gates_tok, gates_t, proj, dn_norm, _tri_masks())


def _shift_down(y):
    row = lax.broadcasted_iota(jnp.int32, y.shape, 0)
    return jnp.where(row == 0, 0.0, pltpu.roll(y, 1, axis=0))


def _shift_up(y):
    row = lax.broadcasted_iota(jnp.int32, y.shape, 0)
    return jnp.where(row == y.shape[0] - 1, 0.0, pltpu.roll(y, y.shape[0] - 1, axis=0))


def _lru_gate_rows(xc, wcat, bcat, sp, a_ref, b_ref, rows):
    g = _mm(xc.astype(BF16), wcat) + bcat
    w = LRU_BLOCK_DIM
    for d in range(2):
        r = _sigmoid_tanh(g[:, (2 * d) * w:(2 * d + 1) * w])
        i = _sigmoid_tanh(g[:, (2 * d + 1) * w:(2 * d + 2) * w])
        a = jnp.exp(-LRU_C * r * sp[d:d + 1, :])
        a_ref[d, rows, :] = a
        b_ref[d, rows, :] = jnp.sqrt(1.0 - a * a) * (i * xc)


def _seg_rows(j, steps, stride_rows, colmajor):
    if colmajor:
        start = (j % GRID_ROWS) * GRID_W + j // GRID_ROWS
        return pl.ds(start, LRU_SEGS, stride=GRID_W // LRU_SEGS)
    return pl.ds(j, LRU_SEGS, stride=steps)


def _carry_chain(h_end, p_end, c0, reverse):
    row = lax.broadcasted_iota(jnp.int32, h_end.shape, 0)
    carries = jnp.zeros_like(h_end)
    c = c0
    order = range(LRU_SEGS - 1, -1, -1) if reverse else range(LRU_SEGS)
    for gseg in order:
        carries = jnp.where(row == gseg, c, carries)
        c = h_end[gseg:gseg + 1, :] + p_end[gseg:gseg + 1, :] * c
    return carries, c


def _lru_kernel(x_ref, xc_in_ref, gate_ref, cw_ref, cb_ref, wcat_ref, bcat_ref, lam_ref, o_ref,
                xc_ref, xs_ref, xcc_ref, a_ref, b_ref, ac_ref, bc_ref, hl_ref, pp_ref, hs_ref):
    cw = cw_ref[...]
    cb = cb_ref[...]
    wcat = wcat_ref[...]
    bcat = bcat_ref[...]
    sp = _softplus(-lam_ref[...])
    gw = GRID_W

    _fill_halo_scratch(xc_in_ref, xs_ref)
    for i in range(CTX_LEN // DN_CHUNK):
        rows = pl.ds(i * DN_CHUNK, DN_CHUNK)
        xcc_ref[rows, :] = _conv4_chunk(xs_ref, i, cw) + cb
    _lru_gate_rows(xcc_ref[...], wcat, bcat, sp, ac_ref, bc_ref, pl.ds(0, CTX_LEN))

    csteps = CTX_LEN // LRU_SEGS
    zero = jnp.zeros((LRU_SEGS, LRU_BLOCK_DIM), F32)
    one = jnp.ones((LRU_SEGS, LRU_BLOCK_DIM), F32)

    def ctx_body(j, carry):
        hf, pf, hb, pb = carry
        rf = _seg_rows(j, csteps, None, False)
        a = ac_ref[0, rf, :]
        hf = a * hf + bc_ref[0, rf, :]
        pf = a * pf
        rb = _seg_rows(csteps - 1 - j, csteps, None, False)
        a = ac_ref[1, rb, :]
        hb = a * hb + bc_ref[1, rb, :]
        pb = a * pb
        return hf, pf, hb, pb

    hf, pf, hb, pb = lax.fori_loop(0, csteps, ctx_body, (zero, one, zero, one))
    zrow = jnp.zeros((1, LRU_BLOCK_DIM), F32)
    _, h0_f = _carry_chain(hf, pf, zrow, False)
    _, h0_b = _carry_chain(hb, pb, zrow, True)

    def slab(r):
        return x_ref[pl.ds(pl.multiple_of(r * gw, gw), gw), :]

    def conv_slab(r, xm2, xm1, x0, xp1):
        xc_ref[pl.ds(pl.multiple_of(r * gw, gw), gw), :] = (
            cb + cw[0:1] * xm2 + cw[1:2] * xm1 + cw[2:3] * x0 + cw[3:4] * xp1)

    last = GRID_ROWS - 1
    conv_slab(0, _shift_down(slab(last - 1)), _shift_down(slab(last)), slab(0), slab(1))
    conv_slab(1, _shift_down(slab(last)), slab(0), slab(1), slab(2))
    conv_slab(last, slab(last - 2), slab(last - 1), slab(last), _shift_up(slab(0)))

    def conv_body(r, _):
        conv_slab(r, slab(r - 2), slab(r - 1), slab(r), slab(r + 1))
        return 0

    lax.fori_loop(2, last, conv_body, 0)

    gate_rows = 256

    def gate_body(i, _):
        rows = pl.ds(pl.multiple_of(i * gate_rows, gate_rows), gate_rows)
        _lru_gate_rows(xc_ref[rows, :], wcat, bcat, sp, a_ref, b_ref, rows)
        return 0

    lax.fori_loop(0, SEQ // gate_rows, gate_body, 0)

    steps = SEQ // LRU_SEGS

    seg_stride = GRID_W // LRU_SEGS
    unroll = 8

    def seg_rows(wq, r):
        return pl.ds(r * GRID_W + wq, LRU_SEGS, stride=seg_stride)

    def scan_col(wq, carry):
        def scan_rows(rr, carry):
            hf, pf, hb, pb = carry
            for k in range(unroll):
                r = rr * unroll + k
                rf = seg_rows(wq, r)
                a = a_ref[0, rf, :]
                hf = a * hf + b_ref[0, rf, :]
                pf = a * pf
                hl_ref[0, wq * GRID_ROWS + r] = hf
                pp_ref[0, wq * GRID_ROWS + r] = pf
                wqb = seg_stride - 1 - wq
                rb = GRID_ROWS - 1 - r
                rbk = seg_rows(wqb, rb)
                a = a_ref[1, rbk, :]
                hb = a * hb + b_ref[1, rbk, :]
                pb = a * pb
                hl_ref[1, wqb * GRID_ROWS + rb] = hb
                pp_ref[1, wqb * GRID_ROWS + rb] = pb
            return hf, pf, hb, pb

        return lax.fori_loop(0, GRID_ROWS // unroll, scan_rows, carry)

    hf, pf, hb, pb = lax.fori_loop(0, seg_stride, scan_col, (zero, one, zero, one))
    c_f, _ = _carry_chain(hf, pf, h0_f, False)
    c_b, _ = _carry_chain(hb, pb, h0_b, True)

    def fix_col(wq, _):
        def fix_rows(rr, _):
            for k in range(unroll):
                r = rr * unroll + k
                j = wq * GRID_ROWS + r
                h = hl_ref[0, j] + pp_ref[0, j] * c_f + hl_ref[1, j] + pp_ref[1, j] * c_b
                hs_ref[seg_rows(wq, r), :] = h
            return 0

        return lax.fori_loop(0, GRID_ROWS // unroll, fix_rows, 0)

    lax.fori_loop(0, seg_stride, fix_col, 0)

    def out_body(i, _):
        rows = pl.ds(pl.multiple_of(i * gate_rows, gate_rows), gate_rows)
        o_ref[rows, :] = (hs_ref[rows, :] * _silu(gate_ref[rows, :])).astype(BF16)
        return 0

    lax.fori_loop(0, SEQ // gate_rows, out_body, 0)


def _lru(proj, conv_w, conv_b, wcat, bcat, lam):
    w = LRU_BLOCK_DIM
    steps = SEQ // LRU_SEGS
    ctx_blk0 = BATCH * SEQ // CTX_LEN
    return pl.pallas_call(
        _lru_kernel,
        out_shape=jax.ShapeDtypeStruct((BATCH * SEQ, LRU_WIDTH), BF16),
        grid=(BATCH, LRU_BLOCKS),
        in_specs=[
            pl.BlockSpec((SEQ, w), lambda b, n: (b, P_LRU // w + n)),
            pl.BlockSpec((CTX_LEN, w), lambda b, n: (ctx_blk0 + b, P_LRU // w + n)),
            pl.BlockSpec((SEQ, w), lambda b, n: (b, P_LRG // w + n)),
            pl.BlockSpec((LRU_CONV, w), lambda b, n: (0, n)),
            pl.BlockSpec((1, w), lambda b, n: (0, n)),
            pl.BlockSpec((None, w, 4 * w), lambda b, n: (n, 0, 0)),
            pl.BlockSpec((None, 1, 4 * w), lambda b, n: (n, 0, 0)),
            pl.BlockSpec((None, 2, w), lambda b, n: (n, 0, 0)),
        ],
        out_specs=pl.BlockSpec((SEQ, w), lambda b, n: (b, n)),
        scratch_shapes=[
            pltpu.VMEM((SEQ, w), F32),
            pltpu.VMEM((CTX_LEN + 2 * HALO, w), F32),
            pltpu.VMEM((CTX_LEN, w), F32),
            pltpu.VMEM((2, SEQ, w), F32),
            pltpu.VMEM((2, SEQ, w), F32),
            pltpu.VMEM((2, CTX_LEN, w), F32),
            pltpu.VMEM((2, CTX_LEN, w), F32),
            pltpu.VMEM((2, steps, LRU_SEGS, w), F32),
            pltpu.VMEM((2, steps, LRU_SEGS, w), F32),
            pltpu.VMEM((SEQ, w), F32),
        ],
        compiler_params=_params("arbitrary", "arbitrary"),
        name="rglru",
    )(proj, proj, proj, conv_w, conv_b, wcat, bcat, lam)


def _out_proj_kernel(ya_ref, yb_ref, wa_ref, wb_ref, x_ref, g_ref, o_ref, wcast_ref):
    @pl.when(pl.program_id(1) == 0)
    def _():
        def cast_body(r, _):
            rows = pl.ds(pl.multiple_of(r * W_CAST_ROWS, W_CAST_ROWS), W_CAST_ROWS)
            wcast_ref[0, rows, :] = wa_ref[rows, :].astype(BF16)
            wcast_ref[1, rows, :] = wb_ref[rows, :].astype(BF16)
            return 0

        lax.fori_loop(0, wa_ref.shape[0] // W_CAST_ROWS, cast_body, 0)

    acc = _mm(ya_ref[...], wcast_ref[0]) + _mm(yb_ref[...], wcast_ref[1])
    o_ref[...] = x_ref[...] + g_ref[...] * acc


def _out_proj(ya, yb, ya_blk, yb_blk, w, x2d, gate3, tm=1024, tn=512):
    m, dm = x2d.shape
    kh = w.shape[0] // 2
    per_batch = SEQ // tm
    return pl.pallas_call(
        _out_proj_kernel,
        out_shape=jax.ShapeDtypeStruct((m, dm), F32),
        grid=(dm // tn, m // tm),
        in_specs=[
            pl.BlockSpec((tm, kh), lambda j, i: (i, ya_blk)),
            pl.BlockSpec((tm, kh), lambda j, i: (i, yb_blk)),
            pl.BlockSpec((kh, tn), lambda j, i: (0, j)),
            pl.BlockSpec((kh, tn), lambda j, i: (1, j)),
            pl.BlockSpec((tm, tn), lambda j, i: (i, j)),
            pl.BlockSpec((None, 1, tn), lambda j, i: (i // per_batch, 0, j)),
        ],
        out_specs=pl.BlockSpec((tm, tn), lambda j, i: (i, j)),
        scratch_shapes=[pltpu.VMEM((2, kh, tn), BF16)],
        compiler_params=_params("arbitrary", "arbitrary"),
        name="out_proj",
    )(ya, yb, w, w, x2d, gate3)


def _sc_proj_kernel(a_ref, wb_ref, wc_ref, wx_ref, wg_ref, cw_ref, o_ref, wcast_ref, z_ref, bg_ref):
    @pl.when(pl.program_id(1) == 0)
    def _():
        def cast_body(r, _):
            rows = pl.ds(pl.multiple_of(r * W_CAST_ROWS, W_CAST_ROWS), W_CAST_ROWS)
            for gi, w_ref in enumerate((wb_ref, wc_ref, wx_ref, wg_ref)):
                wcast_ref[gi, rows, :] = w_ref[rows, :].astype(BF16)
            return 0

        lax.fori_loop(0, wb_ref.shape[0] // W_CAST_ROWS, cast_body, 0)

    hn = a_ref[...]
    z_ref[...] = _mm(hn, wcast_ref[1]) * _mm(hn, wcast_ref[2])
    bg_ref[...] = _mm(hn, wcast_ref[0]) * _silu(_mm(hn, wcast_ref[3]))
    cw = cw_ref[...]
    gw = GRID_W

    def body(r, _):
        rows = pl.ds(pl.multiple_of(r * gw, gw), gw)
        z = z_ref[rows, :]
        zc = cw[0:1] * _shift_down(z) + cw[1:2] * z + cw[2:3] * _shift_up(z)
        o_ref[rows, :] = (bg_ref[rows, :] * zc).astype(BF16)
        return 0

    lax.fori_loop(0, a_ref.shape[0] // gw, body, 0)


def _sc_proj(hn, w, conv_w, tm=512, tc=256):
    m, dm = hn.shape
    nt = SC_WIDTH // tc
    return pl.pallas_call(
        _sc_proj_kernel,
        out_shape=jax.ShapeDtypeStruct((m, SC_WIDTH), BF16),
        grid=(nt, m // tm),
        in_specs=[
            pl.BlockSpec((tm, dm), lambda j, i: (i, 0)),
            pl.BlockSpec((dm, tc), lambda j, i: (0, j)),
            pl.BlockSpec((dm, tc), lambda j, i: (0, nt + j)),
            pl.BlockSpec((dm, tc), lambda j, i: (0, 2 * nt + j)),
            pl.BlockSpec((dm, tc), lambda j, i: (0, 3 * nt + j)),
            pl.BlockSpec((SC_CONV, tc), lambda j, i: (0, j)),
        ],
        out_specs=pl.BlockSpec((tm, tc), lambda j, i: (i, j)),
        scratch_shapes=[pltpu.VMEM((4, dm, tc), BF16), pltpu.VMEM((tm, tc), F32), pltpu.VMEM((tm, tc), F32)],
        compiler_params=_params("arbitrary", "arbitrary"),
        name="sc_proj",
    )(hn, w, w, w, w, conv_w)


def _final_norm_kernel(x_ref, w_ref, o_ref):
    w = w_ref[...]

    def body(r, _):
        rows = pl.ds(pl.multiple_of(r * NORM_ROWS, NORM_ROWS), NORM_ROWS)
        x = x_ref[rows, :]
        ms = jnp.mean(x * x, axis=-1, keepdims=True)
        o_ref[rows, :] = x * lax.rsqrt(ms + EPS) * w
        return 0

    lax.fori_loop(0, x_ref.shape[0] // NORM_ROWS, body, 0, unroll=4)


def _final_norm(x2d, w, tm=256):
    m, dm = x2d.shape
    return pl.pallas_call(
        _final_norm_kernel,
        out_shape=jax.ShapeDtypeStruct((m, dm), F32),
        grid=(m // tm,),
        in_specs=[pl.BlockSpec((tm, dm), lambda i: (i, 0)), pl.BlockSpec((1, dm), lambda i: (0, 0))],
        out_specs=pl.BlockSpec((tm, dm), lambda i: (i, 0)),
        compiler_params=_params("arbitrary"),
        name="final_norm",
    )(x2d, w)


def _gate_lane_vector(p):
    z = jnp.zeros((DN_HEADS,), F32)
    cols = jnp.stack([z, z, p[0], p[1], p[0], p[1], z, z], axis=1)
    return cols.reshape(1, LANES).astype(F32)


def _gate_lane_permutation():
    p = np.zeros((LANES, LANES), np.float32)
    for h in range(DN_HEADS):
        for slot, src in enumerate([h, DN_HEADS + h, 2 * DN_HEADS + h, 3 * DN_HEADS + h, 2 * DN_HEADS + h, 3 * DN_HEADS + h]):
            p[src, h * GATE_SLOTS + slot] = 1.0
    return jnp.asarray(p, BF16)


def kernel(x, c, ctx, c_ctx, mod_w, mod_b, norm_w, ab_w_in, ab_qkv_conv, ab_a_log, ab_dt_bias, ab_dn_norm,
           ab_lru_conv_w, ab_lru_conv_b, ab_lru_w_r, ab_lru_b_r, ab_lru_w_i, ab_lru_b_i, ab_lru_lambda, ab_w_out,
           sc_w_in, sc_conv, sc_w_out, final_norm_w):
    dm = D_MODEL
    x2d = x.reshape(BATCH * SEQ, dm)
    ctx2d = ctx.reshape(BATCH * CTX_LEN, dm)

    c8 = jnp.concatenate([c, c_ctx[None, :], jnp.zeros((8 - BATCH - 1, dm), F32)], axis=0)
    mod = _modulation(c8, mod_w, mod_b)
    shift = [mod[l, :, :dm].reshape(8, 1, dm) for l in range(2)]
    scale = [mod[l, :, dm:2 * dm].reshape(8, 1, dm) for l in range(2)]
    gate = [mod[l, :, 2 * dm:].reshape(8, 1, dm) for l in range(2)]

    w_in_t = jnp.swapaxes(ab_w_in[0], 0, 1)
    nw0 = norm_w[0].reshape(1, dm)
    hn = _norm_mod(x2d, ctx2d, nw0, scale[0], shift[0])
    proj = _proj0(hn, w_in_t)
    ba = _gate_proj(hn, w_in_t)

    n_lat = BATCH * SEQ
    ba_all = jnp.concatenate([ba[n_lat:].reshape(BATCH, CTX_LEN, LANES), ba[:n_lat].reshape(BATCH, SEQ, LANES)], axis=1)
    gates_tok = _gates(ba_all.reshape(BATCH * ALL_LEN, LANES), _gate_lane_vector(ab_a_log[0]),
                       _gate_lane_vector(ab_dt_bias[0])).reshape(BATCH, ALL_LEN, LANES)
    gates_t = gates_tok.transpose(0, 2, 1)

    y_dn = _deltanet(proj, ab_qkv_conv[0], gates_tok, gates_t, ab_dn_norm[0].reshape(1, DN_HEAD_DIM))

    wcat = jnp.concatenate([ab_lru_w_r[0, 0], ab_lru_w_i[0, 0], ab_lru_w_r[0, 1], ab_lru_w_i[0, 1]],
                           axis=-1).astype(BF16)
    bl = LRU_BLOCK_DIM
    bcat = jnp.concatenate([ab_lru_b_r[0, 0].reshape(LRU_BLOCKS, 1, bl), ab_lru_b_i[0, 0].reshape(LRU_BLOCKS, 1, bl),
                            ab_lru_b_r[0, 1].reshape(LRU_BLOCKS, 1, bl), ab_lru_b_i[0, 1].reshape(LRU_BLOCKS, 1, bl)],
                           axis=-1)
    lam = ab_lru_lambda[0].reshape(2, LRU_BLOCKS, bl).transpose(1, 0, 2)
    y_lru = _lru(proj, ab_lru_conv_w[0], ab_lru_conv_b[0].reshape(1, LRU_WIDTH), wcat, bcat, lam)

    x1 = _out_proj(y_dn, y_lru, 0, 0, ab_w_out[0], x2d, gate[0])

    hn1 = _norm_mod(x1, None, norm_w[1].reshape(1, dm), scale[1], shift[1])
    y1 = _sc_proj(hn1, sc_w_in[0], sc_conv[0])
    x2 = _out_proj(y1, y1, 0, 1, sc_w_out[0], x1, gate[1])

    return _final_norm(x2, final_norm_w.reshape(1, dm)).reshape(BATCH, SEQ, dm)
```

```python
import functools

import jax
import jax.numpy as jnp
import numpy as np
from jax import lax
from jax.experimental import pallas as pl
from jax.experimental.pallas import tpu as pltpu

F32 = jnp.float32
BF16 = jnp.bfloat16

D_MODEL = 4096
BATCH = 2
SEQ = 4096
CTX_LEN = 256
GRID_W = 64
GRID_ROWS = SEQ // GRID_W
EPS = 1e-6

DN_HEADS = 16
DN_HEAD_DIM = 128
DN_WIDTH = DN_HEADS * DN_HEAD_DIM
DN_CONV = 4
DN_CHUNK = 64

LRU_WIDTH = 2048
LRU_BLOCKS = 16
LRU_BLOCK_DIM = LRU_WIDTH // LRU_BLOCKS
LRU_CONV = 4
LRU_C = 8.0

SC_WIDTH = D_MODEL
SC_CONV = 3

OFF_LRU = 3 * DN_WIDTH
OFF_BETA = OFF_LRU + LRU_WIDTH
OFF_ALPHA = OFF_BETA + 2 * DN_HEADS
AB_STATE = OFF_ALPHA + 2 * DN_HEADS

P_K = DN_WIDTH
P_V = 2 * DN_WIDTH
P_LRU = 3 * DN_WIDTH
P_DNG = P_LRU + LRU_WIDTH
P_LRG = P_DNG + DN_WIDTH
P_WIDTH = P_LRG + LRU_WIDTH
LANES = 128
GATE_SLOTS = LANES // DN_HEADS

N_CTX_CHUNKS = CTX_LEN // DN_CHUNK
N_SEQ_CHUNKS = SEQ // DN_CHUNK
N_SLOTS = N_CTX_CHUNKS + N_SEQ_CHUNKS
ALL_LEN = CTX_LEN + SEQ

LRU_SEGS = 8
NEG_BIG = -1e30

VMEM_LIMIT = 56 * 1024 * 1024


def _params(*sem):
    return pltpu.CompilerParams(dimension_semantics=sem, vmem_limit_bytes=VMEM_LIMIT)


def _mm(a, b):
    return jnp.dot(a, b, preferred_element_type=F32)


def _mo(v, m):
    return v if isinstance(v, int) else pl.multiple_of(v, m)


def _silu(x):
    return x * jax.nn.sigmoid(x)


def _sigmoid_tanh(x):
    return 0.5 * jnp.tanh(0.5 * x) + 0.5


def _softplus(x):
    return jnp.maximum(x, 0.0) + jnp.log1p(jnp.exp(-jnp.abs(x)))


def _mod_kernel(c_ref, w_ref, b_ref, o_ref):
    s = _silu(c_ref[...]).astype(BF16)
    o_ref[...] = _mm(s, w_ref[...].astype(BF16)) + b_ref[...]


def _modulation(c8, mod_w, mod_b):
    depth, dm, n = mod_w.shape
    tn = 512
    return pl.pallas_call(
        _mod_kernel,
        out_shape=jax.ShapeDtypeStruct((depth, 8, n), F32),
        grid=(depth, n // tn),
        in_specs=[
            pl.BlockSpec((8, dm), lambda l, j: (0, 0)),
            pl.BlockSpec((None, dm, tn), lambda l, j: (l, 0, j)),
            pl.BlockSpec((None, 1, tn), lambda l, j: (l, 0, j)),
        ],
        out_specs=pl.BlockSpec((None, 8, tn), lambda l, j: (l, 0, j)),
        compiler_params=_params("arbitrary", "arbitrary"),
        name="modulation",
    )(c8, mod_w, mod_b.reshape(depth, 1, n))


NORM_ROWS = 8


def _norm_rows(x_ref, nw_ref, sc_ref, sh_ref, hn_ref):
    tm = x_ref.shape[0]
    wsc = nw_ref[...] * (1.0 + sc_ref[...])
    sh = sh_ref[...]

    def body(r, _):
        rows = pl.ds(pl.multiple_of(r * NORM_ROWS, NORM_ROWS), NORM_ROWS)
        x = x_ref[rows, :]
        ms = jnp.mean(x * x, axis=-1, keepdims=True)
        hn_ref[rows, :] = (x * lax.rsqrt(ms + EPS) * wsc + sh).astype(BF16)
        return 0

    lax.fori_loop(0, tm // NORM_ROWS, body, 0, unroll=4)


def _norm_mod_kernel(x_ref, c_ref, nw_ref, sc_ref, sh_ref, o_ref, *, n_x_tiles):
    @pl.when(pl.program_id(0) < n_x_tiles)
    def _():
        _norm_rows(x_ref, nw_ref, sc_ref, sh_ref, o_ref)

    @pl.when(pl.program_id(0) >= n_x_tiles)
    def _():
        _norm_rows(c_ref, nw_ref, sc_ref, sh_ref, o_ref)


def _norm_mod(x2d, c2d, nw, sc3, sh3, tm=512):
    m, dm = x2d.shape
    mc = 0 if c2d is None else c2d.shape[0]
    c2d = x2d if c2d is None else c2d
    n_x_tiles = m // tm
    per_batch = SEQ // tm
    mod_spec = pl.BlockSpec((None, 1, dm), lambda i: (i // per_batch, 0, 0))
    return pl.pallas_call(
        functools.partial(_norm_mod_kernel, n_x_tiles=n_x_tiles),
        out_shape=jax.ShapeDtypeStruct((m + mc, dm), BF16),
        grid=(n_x_tiles + mc // tm,),
        in_specs=[
            pl.BlockSpec((tm, dm), lambda i: (jnp.minimum(i, n_x_tiles - 1), 0)),
            pl.BlockSpec((tm, dm), lambda i: (jnp.maximum(i - n_x_tiles, 0), 0)),
            pl.BlockSpec((1, dm), lambda i: (0, 0)),
            mod_spec,
            mod_spec,
        ],
        out_specs=pl.BlockSpec((tm, dm), lambda i: (i, 0)),
        compiler_params=_params("arbitrary"),
        name="norm_mod",
    )(x2d, c2d, nw, sc3, sh3)


W_CAST_ROWS = 64


def _proj0_kernel(a_ref, w_ref, wn_ref, o_ref, wb_ref, *, first_shifted_tile, shift):
    j = pl.program_id(0)
    tn, dm = w_ref.shape

    def cast_slabs(fn):
        def body(c, _):
            cols = pl.ds(pl.multiple_of(c * LANES, LANES), LANES)
            wb_ref[cols, :] = fn(cols).T.astype(BF16)
            return 0

        lax.fori_loop(0, dm // LANES, body, 0)

    @pl.when((pl.program_id(1) == 0) & (j < first_shifted_tile))
    def _():
        cast_slabs(lambda cols: w_ref[:, cols])

    @pl.when((pl.program_id(1) == 0) & (j >= first_shifted_tile))
    def _():
        cast_slabs(lambda cols: jnp.concatenate([w_ref[pl.ds(shift, tn - shift), cols], wn_ref[pl.ds(0, shift), cols]],
                                                axis=0))

    o_ref[...] = _mm(a_ref[...], wb_ref[...])


def _proj0(hn, w_in_t, tm=544, tn=1024):
    m, dm = hn.shape
    shift = AB_STATE - OFF_BETA
    return pl.pallas_call(
        functools.partial(_proj0_kernel, first_shifted_tile=OFF_BETA // tn, shift=shift),
        out_shape=jax.ShapeDtypeStruct((m, P_WIDTH), F32),
        grid=(P_WIDTH // tn, m // tm),
        in_specs=[
            pl.BlockSpec((tm, dm), lambda j, i: (i, 0)),
            pl.BlockSpec((tn, dm), lambda j, i: (j, 0), pipeline_mode=pl.Buffered(1)),
            pl.BlockSpec((shift, dm), lambda j, i: ((j + 1) * (tn // shift), 0)),
        ],
        out_specs=pl.BlockSpec((tm, tn), lambda j, i: (i, j)),
        scratch_shapes=[pltpu.VMEM((dm, tn), BF16)],
        compiler_params=_params("arbitrary", "arbitrary"),
        name="proj0",
    )(hn, w_in_t, w_in_t)


def _gate_proj_kernel(a_ref, w_ref, perm_ref, o_ref, wb_ref):
    @pl.when(pl.program_id(0) == 0)
    def _():
        wb_ref[...] = _mm(perm_ref[...], w_ref[...].astype(BF16)).astype(BF16)

    o_ref[...] = lax.dot_general(a_ref[...], wb_ref[...], (((1,), (1,)), ((), ())), preferred_element_type=F32)


def _gate_proj(a, w_in_t, tm=512):
    m, dm = a.shape
    return pl.pallas_call(
        _gate_proj_kernel,
        out_shape=jax.ShapeDtypeStruct((m, LANES), F32),
        grid=(m // tm,),
        in_specs=[
            pl.BlockSpec((tm, dm), lambda i: (i, 0)),
            pl.BlockSpec((LANES, dm), lambda i: (OFF_BETA // LANES, 0)),
            pl.BlockSpec((LANES, LANES), lambda i: (0, 0)),
        ],
        out_specs=pl.BlockSpec((tm, LANES), lambda i: (i, 0)),
        scratch_shapes=[pltpu.VMEM((LANES, dm), BF16)],
        compiler_params=_params("arbitrary"),
        name="gate_proj",
    )(a, w_in_t, _gate_row_selection())


def _split3(a):
    a1 = a.astype(BF16)
    r1 = a - a1.astype(F32)
    a2 = r1.astype(BF16)
    a3 = (r1 - a2.astype(F32)).astype(BF16)
    return a1, a2, a3


def _gates_kernel(ba_ref, alog_ref, dt_ref, o_ref):
    tm = ba_ref.shape[0]
    r = lax.broadcasted_iota(jnp.int32, (DN_CHUNK, DN_CHUNK), 0)
    c = lax.broadcasted_iota(jnp.int32, (DN_CHUNK, DN_CHUNK), 1)
    lower = (r >= c).astype(BF16)
    upper = (r <= c).astype(BF16)
    ones = jnp.ones((DN_CHUNK, DN_CHUNK), BF16)
    slot = lax.broadcasted_iota(jnp.int32, (DN_CHUNK, LANES), 1) % GATE_SLOTS
    neg_a = -jnp.exp(alog_ref[...])
    dt = dt_ref[...]

    def body(i, _):
        rows = pl.ds(pl.multiple_of(i * DN_CHUNK, DN_CHUNK), DN_CHUNK)
        x = ba_ref[rows, :]
        beta = jax.nn.sigmoid(x)
        g = neg_a * _softplus(x + dt)
        g1, g2, g3 = _split3(g)
        cum = _mm(lower, g1) + _mm(lower, g2) + _mm(lower, g3)
        suf = _mm(upper, g1) + _mm(upper, g2) + _mm(upper, g3)
        tot = _mm(ones, g1) + _mm(ones, g2) + _mm(ones, g3)
        o_ref[rows, :] = jnp.where(slot < 2, beta, jnp.where(slot == 2, cum, jnp.where(slot == 3, suf, tot)))
        return 0

    lax.fori_loop(0, tm // DN_CHUNK, body, 0)


def _gates(ba, alog_l, dt_l, tm=512):
    m = ba.shape[0]
    return pl.pallas_call(
        _gates_kernel,
        out_shape=jax.ShapeDtypeStruct((m, LANES), F32),
        grid=(m // tm,),
        in_specs=[
            pl.BlockSpec((tm, LANES), lambda i: (i, 0)),
            pl.BlockSpec((1, LANES), lambda i: (0, 0)),
            pl.BlockSpec((1, LANES), lambda i: (0, 0)),
        ],
        out_specs=pl.BlockSpec((tm, LANES), lambda i: (i, 0)),
        compiler_params=_params("arbitrary"),
        name="dn_gates",
    )(ba, alog_l, dt_l)


HALO = 8


def _fill_halo_scratch(x_ref, xs_ref):
    length, width = x_ref.shape
    zeros = jnp.zeros((HALO, width), F32)
    xs_ref[pl.ds(0, HALO), :] = zeros
    xs_ref[pl.ds(HALO + length, HALO), :] = zeros

    def body(i, _):
        rows = pl.ds(pl.multiple_of(i * DN_CHUNK, DN_CHUNK), DN_CHUNK)
        xs_ref[pl.ds(pl.multiple_of(i * DN_CHUNK + HALO, HALO), DN_CHUNK), :] = x_ref[rows, :]
        return 0

    lax.fori_loop(0, length // DN_CHUNK, body, 0)


def _conv4_chunk(xs_ref, i, w):
    v = xs_ref[pl.ds(pl.multiple_of(i * DN_CHUNK, DN_CHUNK), DN_CHUNK + 2 * HALO), :]
    n = DN_CHUNK
    return (w[0:1] * v[HALO - 2:HALO - 2 + n] + w[1:2] * v[HALO - 1:HALO - 1 + n]
            + w[2:3] * v[HALO:HALO + n] + w[3:4] * v[HALO + 1:HALO + 1 + n])


DN_GROUP = 16
N_TRI_LEVELS = 6
M_STRICT, M_INCL, M_EYE, M_JOIN = 0, 1, 2, 3


def _tri_masks():
    n = DN_CHUNK
    r = np.arange(n)[:, None].repeat(2 * n, axis=1)
    c = np.arange(2 * n)[None, :].repeat(n, axis=0) % n
    upper = np.arange(2 * n)[None, :] >= n
    rr = np.where(upper, c, r)
    cc = np.where(upper, r, c)
    out = [rr > cc, rr >= cc, rr == cc]
    for lvl in range(N_TRI_LEVELS):
        s = 1 << lvl
        out.append((rr // (2 * s) == cc // (2 * s)) & (rr % (2 * s) >= s) & (cc % (2 * s) < s))
    return jnp.asarray(np.stack(out).astype(np.float32))


def _block_diag2(x):
    lane = lax.broadcasted_iota(jnp.int32, x.shape, 1)
    zero = jnp.zeros_like(x)
    return jnp.concatenate([jnp.where(lane < DN_CHUNK, x, zero), jnp.where(lane >= DN_CHUNK, x, zero)], axis=0)


def _packed_tri_inverse(a_list, m_ref):
    d_list = [m_ref[M_EYE] - m_ref[M_JOIN] * a for a in a_list]
    for lvl in range(1, N_TRI_LEVELS):
        join = m_ref[M_JOIN + lvl]
        l_bd = [_block_diag2((join * a).astype(BF16)) for a in a_list]
        d_b = [d.astype(BF16) for d in d_list]
        e_b = [_mm(db, lb).astype(BF16) for db, lb in zip(d_b, l_bd)]
        d_list = [d - _mm(eb, _block_diag2(db)) for d, eb, db in zip(d_list, e_b, d_b)]
    return d_list


def _dn_local_group(h, srcs, g_all, gt_all, slot0, m_ref, nq_ref, c_ref, ge_ref, acc_ref):
    n = DN_CHUNK
    lane = lax.broadcasted_iota(jnp.int32, (n, LANES), 1)
    lane1 = lax.broadcasted_iota(jnp.int32, (1, LANES), 1)
    upper = lane >= n
    strict = m_ref[M_STRICT]
    incl = m_ref[M_INCL] > 0.5
    nt = (((1,), (1,)), ((), ()))
    cols, a_list, decays, kqs = [], [], [], []
    for e, (kc, vc, qc) in enumerate(srcs):
        g = g_all[e * n:(e + 1) * n]

        def col(s, g=g):
            return jnp.sum(jnp.where(lane == h * GATE_SLOTS + s, g, 0.0), axis=1, keepdims=True)

        cs = [col(s) for s in range(6)]
        t0 = (e // 2) * LANES
        row_f = gt_all[2:3, t0:t0 + LANES]
        row_b = gt_all[3:4, t0:t0 + LANES]
        if e % 2 == 0:
            g_row = jnp.where(lane1 < n, row_f, pltpu.roll(row_b, n, axis=1))
        else:
            g_row = jnp.where(lane1 < n, pltpu.roll(row_f, n, axis=1), row_b)
        b_pack = jnp.where(upper, cs[1], cs[0])
        g_pack = jnp.where(upper, cs[3], cs[2])
        decay = jnp.exp(jnp.where(incl, g_pack - g_row, NEG_BIG))
        kb = kc.astype(BF16)
        lhs = kb if qc is None else jnp.concatenate([kb, qc.astype(BF16)], axis=0)
        kq = lax.dot_general(lhs, jnp.concatenate([kb, kb], axis=0), nt, preferred_element_type=F32)
        cols.append(cs)
        decays.append(decay)
        kqs.append(kq)
        a_list.append(strict * (b_pack * kq[:n] * decay))
    t_inv = _packed_tri_inverse(a_list, m_ref)
    for e, (kc, vc, qc) in enumerate(srcs):
        b_f, b_b, g_f, g_b, tot_f, tot_b = cols[e]
        eg = (jnp.exp(g_f), jnp.exp(g_b))
        rhs = jnp.concatenate([jnp.concatenate([kc * (b_f * eg[0]), vc * b_f], axis=1),
                               jnp.concatenate([kc * (b_b * eg[1]), vc * b_b], axis=1)], axis=0)
        wu = _mm(_block_diag2(t_inv[e].astype(BF16)), rhs.astype(BF16))
        slot = slot0 + e
        nq_rows = _mo(slot * 3 * n, n)
        for d in range(2):
            g_col, tot_col = (g_f, tot_f) if d == 0 else (g_b, tot_b)
            ke_t = (kc * jnp.exp(tot_col - g_col)).T.astype(BF16)
            wub = wu[d * n:(d + 1) * n].astype(BF16)
            ge_ref[d, pl.ds(_mo(slot * 8, 8), 8), :] = jnp.broadcast_to(jnp.exp(tot_col[:8]), (8, LANES))
            if qc is None:
                nc = _mm(ke_t, wub)
            else:
                a_qk = (kqs[e][n:] * decays[e])[:, d * n:(d + 1) * n].astype(BF16)
                nc = _mm(jnp.concatenate([ke_t, a_qk], axis=0), wub)
                nq_ref[d, pl.ds(nq_rows + 2 * n, n), :] = (qc * eg[d] - nc[2 * n:, :DN_HEAD_DIM]).astype(BF16)
                rows = pl.ds(_mo((slot - N_CTX_CHUNKS) * n, n), n)
                acc_ref[rows, :] += nc[2 * n:, DN_HEAD_DIM:]
            nq_ref[d, pl.ds(nq_rows, 2 * n), :] = nc[:2 * n, :DN_HEAD_DIM].astype(BF16)
            c_ref[d, pl.ds(_mo(slot * 2 * n, 2 * n), 2 * n), :] = nc[:2 * n, DN_HEAD_DIM:]


def _dn_state_step(d, slot, s, nq_ref, c_ref, ge_ref, with_output):
    n = DN_CHUNK
    nq_rows = _mo(slot * 3 * n, n)
    ge = jnp.tile(ge_ref[d, pl.ds(_mo(slot * 8, 8), 8), :], (DN_HEAD_DIM // 8, 1))
    c = c_ref[d, pl.ds(_mo(slot * 2 * n, 2 * n), 2 * n), :]
    r = _mm(nq_ref[d, pl.ds(nq_rows, (3 if with_output else 2) * n), :], s.astype(BF16))
    return s * ge - r[:2 * n] + c, (r[2 * n:] if with_output else None)


def _conv_silu_chunk(ref, w, start, length, l2_scale):
    n = DN_CHUNK
    lo, hi = (max, min) if isinstance(start, int) else (jnp.maximum, jnp.minimum)
    prev = ref[pl.ds(_mo(lo(start - HALO, 0), HALO), HALO), :]
    nxt = ref[pl.ds(_mo(hi(start + n, length - HALO), HALO), HALO), :]
    prev = jnp.where(start > 0, prev, 0.0)
    nxt = jnp.where(start + n < length, nxt, 0.0)
    v = jnp.concatenate([prev, ref[pl.ds(_mo(start, n), n), :], nxt], axis=0)
    rows = v.shape[0]
    acc = pltpu.roll(w[0:1] * v, 1, axis=0) + w[1:2] * v
    acc = pltpu.roll(acc, 1, axis=0) + w[2:3] * v + pltpu.roll(w[3:4] * v, rows - 1, axis=0)
    a = _silu(acc[HALO:HALO + n])
    if l2_scale is None:
        return a
    return a * (lax.rsqrt(jnp.sum(a * a, axis=-1, keepdims=True) + EPS) * l2_scale)


def _deltanet_kernel(q_ref, k_ref, v_ref, kc_ref, vc_ref, wq_ref, wk_ref, wv_ref, g_ref, gt_ref, gate_ref, nw_ref,
                     m_ref, o_ref, nq_ref, c_ref, ge_ref, acc_ref):
    h = pl.program_id(1)
    n = DN_CHUNK
    gl = DN_GROUP * n
    acc_ref[...] = jnp.zeros_like(acc_ref)
    wq, wk, wv = wq_ref[...], wk_ref[...], wv_ref[...]

    srcs = [(_conv_silu_chunk(kc_ref, wk, e * n, CTX_LEN, 1.0), _conv_silu_chunk(vc_ref, wv, e * n, CTX_LEN, None), None)
            for e in range(N_CTX_CHUNKS)]
    _dn_local_group(h, srcs, g_ref[pl.ds(0, CTX_LEN), :], gt_ref[:, pl.ds(0, CTX_LEN)], 0,
                    m_ref, nq_ref, c_ref, ge_ref, acc_ref)

    def local_body(p, _):
        base = pl.multiple_of(p * gl, gl)
        gbase = pl.multiple_of(CTX_LEN + p * gl, gl)
        srcs = [(_conv_silu_chunk(k_ref, wk, base + e * n, SEQ, 1.0), _conv_silu_chunk(v_ref, wv, base + e * n, SEQ, None),
                 _conv_silu_chunk(q_ref, wq, base + e * n, SEQ, DN_HEAD_DIM ** -0.5)) for e in range(DN_GROUP)]
        _dn_local_group(h, srcs, g_ref[pl.ds(gbase, gl), :], gt_ref[:, pl.ds(gbase, gl)], N_CTX_CHUNKS + p * DN_GROUP,
                        m_ref, nq_ref, c_ref, ge_ref, acc_ref)
        return 0

    lax.fori_loop(0, N_SEQ_CHUNKS // DN_GROUP, local_body, 0)

    s_f = jnp.zeros((DN_HEAD_DIM, DN_HEAD_DIM), F32)
    s_b = jnp.zeros((DN_HEAD_DIM, DN_HEAD_DIM), F32)
    for i in range(N_CTX_CHUNKS):
        s_f, _ = _dn_state_step(0, i, s_f, nq_ref, c_ref, ge_ref, False)
        s_b, _ = _dn_state_step(1, N_CTX_CHUNKS - 1 - i, s_b, nq_ref, c_ref, ge_ref, False)

    def state_body(i, carry):
        s_f, s_b = carry
        s_f, o_f = _dn_state_step(0, N_CTX_CHUNKS + i, s_f, nq_ref, c_ref, ge_ref, True)
        rows_f = pl.ds(pl.multiple_of(i * n, n), n)
        acc_ref[rows_f, :] += o_f
        jb = N_SEQ_CHUNKS - 1 - i
        s_b, o_b = _dn_state_step(1, N_CTX_CHUNKS + jb, s_b, nq_ref, c_ref, ge_ref, True)
        rows_b = pl.ds(pl.multiple_of(jb * n, n), n)
        acc_ref[rows_b, :] += o_b
        return s_f, s_b

    lax.fori_loop(0, N_SEQ_CHUNKS, state_body, (s_f, s_b), unroll=2)

    nw = nw_ref[...]

    out_rows = 4 * n

    def out_body(i, _):
        rows = pl.ds(pl.multiple_of(i * out_rows, out_rows), out_rows)
        o = acc_ref[rows, :]
        ms = jnp.mean(o * o, axis=-1, keepdims=True)
        o_ref[rows, :] = (o * lax.rsqrt(ms + EPS) * nw * _silu(gate_ref[rows, :])).astype(BF16)
        return 0

    lax.fori_loop(0, SEQ // out_rows, out_body, 0)


def _deltanet(proj, qkv_conv, gates_tok, gates_t, dn_norm):
    n = DN_CHUNK
    hb = DN_WIDTH // DN_HEAD_DIM
    ctx_blk0 = BATCH * SEQ // CTX_LEN
    return pl.pallas_call(
        _deltanet_kernel,
        out_shape=jax.ShapeDtypeStruct((BATCH * SEQ, DN_WIDTH), BF16),
        grid=(BATCH, DN_HEADS),
        in_specs=[
            pl.BlockSpec((SEQ, DN_HEAD_DIM), lambda b, h: (b, h)),
            pl.BlockSpec((SEQ, DN_HEAD_DIM), lambda b, h: (b, hb + h)),
            pl.BlockSpec((SEQ, DN_HEAD_DIM), lambda b, h: (b, 2 * hb + h)),
            pl.BlockSpec((CTX_LEN, DN_HEAD_DIM), lambda b, h: (ctx_blk0 + b, hb + h)),
            pl.BlockSpec((CTX_LEN, DN_HEAD_DIM), lambda b, h: (ctx_blk0 + b, 2 * hb + h)),
            pl.BlockSpec((DN_CONV, DN_HEAD_DIM), lambda b, h: (0, h)),
            pl.BlockSpec((DN_CONV, DN_HEAD_DIM), lambda b, h: (0, hb + h)),
            pl.BlockSpec((DN_CONV, DN_HEAD_DIM), lambda b, h: (0, 2 * hb + h)),
            pl.BlockSpec((None, ALL_LEN, LANES), lambda b, h: (b, 0, 0)),
            pl.BlockSpec((None, GATE_SLOTS, ALL_LEN), lambda b, h: (b, h, 0)),
            pl.BlockSpec((SEQ, DN_HEAD_DIM), lambda b, h: (b, P_DNG // DN_HEAD_DIM + h)),
            pl.BlockSpec((1, DN_HEAD_DIM), lambda b, h: (0, 0)),
            pl.BlockSpec((M_JOIN + N_TRI_LEVELS, n, 2 * n), lambda b, h: (0, 0, 0)),
        ],
        out_specs=pl.BlockSpec((SEQ, DN_HEAD_DIM), lambda b, h: (b, h)),
        scratch_shapes=[
            pltpu.VMEM((2, N_SLOTS * 3 * n, DN_HEAD_DIM), BF16),
            pltpu.VMEM((2, N_SLOTS * 2 * n, DN_HEAD_DIM), F32),
            pltpu.VMEM((2, N_SLOTS * 8, LANES), F32),
            pltpu.VMEM((SEQ, DN_HEAD_DIM), F32),
        ],
        compiler_params=_params("arbitrary", "arbitrary"),
        name="deltanet",
    )(proj, proj, proj, proj, proj, qkv_conv, qkv_conv, qkv_conv, gates_tok, gates_t, proj, dn_norm, _tri_masks())


def _shift_down(y):
    row = lax.broadcasted_iota(jnp.int32, y.shape, 0)
    return jnp.where(row == 0, 0.0, pltpu.roll(y, 1, axis=0))


def _shift_up(y):
    row = lax.broadcasted_iota(jnp.int32, y.shape, 0)
    return jnp.where(row == y.shape[0] - 1, 0.0, pltpu.roll(y, y.shape[0] - 1, axis=0))


def _lru_gate_rows(xc, wcat, bcat, sp, a_ref, b_ref, rows):
    g = _mm(xc.astype(BF16), wcat) + bcat
    w = LRU_BLOCK_DIM
    for d in range(2):
        r = _sigmoid_tanh(g[:, (2 * d) * w:(2 * d + 1) * w])
        i = _sigmoid_tanh(g[:, (2 * d + 1) * w:(2 * d + 2) * w])
        a = jnp.exp(-LRU_C * r * sp[d:d + 1, :])
        a_ref[d, rows, :] = a
        b_ref[d, rows, :] = jnp.sqrt(1.0 - a * a) * (i * xc)


def _seg_rows(j, steps, stride_rows, colmajor):
    if colmajor:
        start = (j % GRID_ROWS) * GRID_W + j // GRID_ROWS
        return pl.ds(start, LRU_SEGS, stride=GRID_W // LRU_SEGS)
    return pl.ds(j, LRU_SEGS, stride=steps)


def _carry_chain(h_end, p_end, c0, reverse):
    row = lax.broadcasted_iota(jnp.int32, h_end.shape, 0)
    carries = jnp.zeros_like(h_end)
    c = c0
    order = range(LRU_SEGS - 1, -1, -1) if reverse else range(LRU_SEGS)
    for gseg in order:
        carries = jnp.where(row == gseg, c, carries)
        c = h_end[gseg:gseg + 1, :] + p_end[gseg:gseg + 1, :] * c
    return carries, c


def _lru_kernel(x_ref, xc_in_ref, gate_ref, cw_ref, cb_ref, wcat_ref, bcat_ref, lam_ref, o_ref,
                xc_ref, xs_ref, xcc_ref, a_ref, b_ref, ac_ref, bc_ref, hl_ref, pp_ref, hs_ref):
    cw = cw_ref[...]
    cb = cb_ref[...]
    wcat = wcat_ref[...]
    bcat = bcat_ref[...]
    sp = _softplus(-lam_ref[...])
    gw = GRID_W

    _fill_halo_scratch(xc_in_ref, xs_ref)
    for i in range(CTX_LEN // DN_CHUNK):
        rows = pl.ds(i * DN_CHUNK, DN_CHUNK)
        xcc_ref[rows, :] = _conv4_chunk(xs_ref, i, cw) + cb
    _lru_gate_rows(xcc_ref[...], wcat, bcat, sp, ac_ref, bc_ref, pl.ds(0, CTX_LEN))

    csteps = CTX_LEN // LRU_SEGS
    zero = jnp.zeros((LRU_SEGS, LRU_BLOCK_DIM), F32)
    one = jnp.ones((LRU_SEGS, LRU_BLOCK_DIM), F32)

    def ctx_body(j, carry):
        hf, pf, hb, pb = carry
        rf = _seg_rows(j, csteps, None, False)
        a = ac_ref[0, rf, :]
        hf = a * hf + bc_ref[0, rf, :]
        pf = a * pf
        rb = _seg_rows(csteps - 1 - j, csteps, None, False)
        a = ac_ref[1, rb, :]
        hb = a * hb + bc_ref[1, rb, :]
        pb = a * pb
        return hf, pf, hb, pb

    hf, pf, hb, pb = lax.fori_loop(0, csteps, ctx_body, (zero, one, zero, one))
    zrow = jnp.zeros((1, LRU_BLOCK_DIM), F32)
    _, h0_f = _carry_chain(hf, pf, zrow, False)
    _, h0_b = _carry_chain(hb, pb, zrow, True)

    def slab(r):
        return x_ref[pl.ds(pl.multiple_of(r * gw, gw), gw), :]

    def conv_slab(r, xm2, xm1, x0, xp1):
        xc_ref[pl.ds(pl.multiple_of(r * gw, gw), gw), :] = (
            cb + cw[0:1] * xm2 + cw[1:2] * xm1 + cw[2:3] * x0 + cw[3:4] * xp1)

    last = GRID_ROWS - 1
    conv_slab(0, _shift_down(slab(last - 1)), _shift_down(slab(last)), slab(0), slab(1))
    conv_slab(1, _shift_down(slab(last)), slab(0), slab(1), slab(2))
    conv_slab(last, slab(last - 2), slab(last - 1), slab(last), _shift_up(slab(0)))

    def conv_body(r, _):
        conv_slab(r, slab(r - 2), slab(r - 1), slab(r), slab(r + 1))
        return 0

    lax.fori_loop(2, last, conv_body, 0)

    gate_rows = 256

    def gate_body(i, _):
        rows = pl.ds(pl.multiple_of(i * gate_rows, gate_rows), gate_rows)
        _lru_gate_rows(xc_ref[rows, :], wcat, bcat, sp, a_ref, b_ref, rows)
        return 0

    lax.fori_loop(0, SEQ // gate_rows, gate_body, 0)

    steps = SEQ // LRU_SEGS

    seg_stride = GRID_W // LRU_SEGS
    unroll = 8

    def seg_rows(wq, r):
        return pl.ds(r * GRID_W + wq, LRU_SEGS, stride=seg_stride)

    def scan_col(wq, carry):
        def scan_rows(rr, carry):
            hf, pf, hb, pb = carry
            for k in range(unroll):
                r = rr * unroll + k
                rf = seg_rows(wq, r)
                a = a_ref[0, rf, :]
                hf = a * hf + b_ref[0, rf, :]
                pf = a * pf
                hl_ref[0, wq * GRID_ROWS + r] = hf
                pp_ref[0, wq * GRID_ROWS + r] = pf
                wqb = seg_stride - 1 - wq
                rb = GRID_ROWS - 1 - r
                rbk = seg_rows(wqb, rb)
                a = a_ref[1, rbk, :]
                hb = a * hb + b_ref[1, rbk, :]
                pb = a * pb
                hl_ref[1, wqb * GRID_ROWS + rb] = hb
                pp_ref[1, wqb * GRID_ROWS + rb] = pb
            return hf, pf, hb, pb

        return lax.fori_loop(0, GRID_ROWS // unroll, scan_rows, carry)

    hf, pf, hb, pb = lax.fori_loop(0, seg_stride, scan_col, (zero, one, zero, one))
    c_f, _ = _carry_chain(hf, pf, h0_f, False)
    c_b, _ = _carry_chain(hb, pb, h0_b, True)

    def fix_col(wq, _):
        def fix_rows(rr, _):
            for k in range(unroll):
                r = rr * unroll + k
                j = wq * GRID_ROWS + r
                h = hl_ref[0, j] + pp_ref[0, j] * c_f + hl_ref[1, j] + pp_ref[1, j] * c_b
                hs_ref[seg_rows(wq, r), :] = h
            return 0

        return lax.fori_loop(0, GRID_ROWS // unroll, fix_rows, 0)

    lax.fori_loop(0, seg_stride, fix_col, 0)

    def out_body(i, _):
        rows = pl.ds(pl.multiple_of(i * gate_rows, gate_rows), gate_rows)
        o_ref[rows, :] = (hs_ref[rows, :] * _silu(gate_ref[rows, :])).astype(BF16)
        return 0

    lax.fori_loop(0, SEQ // gate_rows, out_body, 0)


def _lru(proj, conv_w, conv_b, wcat, bcat, lam):
    w = LRU_BLOCK_DIM
    steps = SEQ // LRU_SEGS
    ctx_blk0 = BATCH * SEQ // CTX_LEN
    return pl.pallas_call(
        _lru_kernel,
        out_shape=jax.ShapeDtypeStruct((BATCH * SEQ, LRU_WIDTH), BF16),
        grid=(BATCH, LRU_BLOCKS),
        in_specs=[
            pl.BlockSpec((SEQ, w), lambda b, n: (b, P_LRU // w + n)),
            pl.BlockSpec((CTX_LEN, w), lambda b, n: (ctx_blk0 + b, P_LRU // w + n)),
            pl.BlockSpec((SEQ, w), lambda b, n: (b, P_LRG // w + n)),
            pl.BlockSpec((LRU_CONV, w), lambda b, n: (0, n)),
            pl.BlockSpec((1, w), lambda b, n: (0, n)),
            pl.BlockSpec((None, w, 4 * w), lambda b, n: (n, 0, 0)),
            pl.BlockSpec((None, 1, 4 * w), lambda b, n: (n, 0, 0)),
            pl.BlockSpec((None, 2, w), lambda b, n: (n, 0, 0)),
        ],
        out_specs=pl.BlockSpec((SEQ, w), lambda b, n: (b, n)),
        scratch_shapes=[
            pltpu.VMEM((SEQ, w), F32),
            pltpu.VMEM((CTX_LEN + 2 * HALO, w), F32),
            pltpu.VMEM((CTX_LEN, w), F32),
            pltpu.VMEM((2, SEQ, w), F32),
            pltpu.VMEM((2, SEQ, w), F32),
            pltpu.VMEM((2, CTX_LEN, w), F32),
            pltpu.VMEM((2, CTX_LEN, w), F32),
            pltpu.VMEM((2, steps, LRU_SEGS, w), F32),
            pltpu.VMEM((2, steps, LRU_SEGS, w), F32),
            pltpu.VMEM((SEQ, w), F32),
        ],
        compiler_params=_params("arbitrary", "arbitrary"),
        name="rglru",
    )(proj, proj, proj, conv_w, conv_b, wcat, bcat, lam)


def _out_proj_kernel(ya_ref, yb_ref, wa_ref, wb_ref, x_ref, g_ref, o_ref, wcast_ref):
    @pl.when(pl.program_id(1) == 0)
    def _():
        def cast_body(r, _):
            rows = pl.ds(pl.multiple_of(r * W_CAST_ROWS, W_CAST_ROWS), W_CAST_ROWS)
            wcast_ref[0, rows, :] = wa_ref[rows, :].astype(BF16)
            wcast_ref[1, rows, :] = wb_ref[rows, :].astype(BF16)
            return 0

        lax.fori_loop(0, wa_ref.shape[0] // W_CAST_ROWS, cast_body, 0)

    acc = _mm(ya_ref[...], wcast_ref[0]) + _mm(yb_ref[...], wcast_ref[1])
    o_ref[...] = x_ref[...] + g_ref[...] * acc


def _out_proj(ya, yb, ya_blk, yb_blk, w, x2d, gate3, tm=1024, tn=512):
    m, dm = x2d.shape
    kh = w.shape[0] // 2
    per_batch = SEQ // tm
    return pl.pallas_call(
        _out_proj_kernel,
        out_shape=jax.ShapeDtypeStruct((m, dm), F32),
        grid=(dm // tn, m // tm),
        in_specs=[
            pl.BlockSpec((tm, kh), lambda j, i: (i, ya_blk)),
            pl.BlockSpec((tm, kh), lambda j, i: (i, yb_blk)),
            pl.BlockSpec((kh, tn), lambda j, i: (0, j)),
            pl.BlockSpec((kh, tn), lambda j, i: (1, j)),
            pl.BlockSpec((tm, tn), lambda j, i: (i, j)),
            pl.BlockSpec((None, 1, tn), lambda j, i: (i // per_batch, 0, j)),
        ],
        out_specs=pl.BlockSpec((tm, tn), lambda j, i: (i, j)),
        scratch_shapes=[pltpu.VMEM((2, kh, tn), BF16)],
        compiler_params=_params("arbitrary", "arbitrary"),
        name="out_proj",
    )(ya, yb, w, w, x2d, gate3)


def _sc_proj_kernel(a_ref, wb_ref, wc_ref, wx_ref, wg_ref, cw_ref, o_ref, wcast_ref, z_ref, bg_ref):
    @pl.when(pl.program_id(1) == 0)
    def _():
        def cast_body(r, _):
            rows = pl.ds(pl.multiple_of(r * W_CAST_ROWS, W_CAST_ROWS), W_CAST_ROWS)
            for gi, w_ref in enumerate((wb_ref, wc_ref, wx_ref, wg_ref)):
                wcast_ref[gi, rows, :] = w_ref[rows, :].astype(BF16)
            return 0

        lax.fori_loop(0, wb_ref.shape[0] // W_CAST_ROWS, cast_body, 0)

    hn = a_ref[...]
    z_ref[...] = _mm(hn, wcast_ref[1]) * _mm(hn, wcast_ref[2])
    bg_ref[...] = _mm(hn, wcast_ref[0]) * _silu(_mm(hn, wcast_ref[3]))
    cw = cw_ref[...]
    gw = GRID_W

    def body(r, _):
        rows = pl.ds(pl.multiple_of(r * gw, gw), gw)
        z = z_ref[rows, :]
        zc = cw[0:1] * _shift_down(z) + cw[1:2] * z + cw[2:3] * _shift_up(z)
        o_ref[rows, :] = (bg_ref[rows, :] * zc).astype(BF16)
        return 0

    lax.fori_loop(0, a_ref.shape[0] // gw, body, 0)


def _sc_proj(hn, w, conv_w, tm=512, tc=256):
    m, dm = hn.shape
    nt = SC_WIDTH // tc
    return pl.pallas_call(
        _sc_proj_kernel,
        out_shape=jax.ShapeDtypeStruct((m, SC_WIDTH), BF16),
        grid=(nt, m // tm),
        in_specs=[
            pl.BlockSpec((tm, dm), lambda j, i: (i, 0)),
            pl.BlockSpec((dm, tc), lambda j, i: (0, j)),
            pl.BlockSpec((dm, tc), lambda j, i: (0, nt + j)),
            pl.BlockSpec((dm, tc), lambda j, i: (0, 2 * nt + j)),
            pl.BlockSpec((dm, tc), lambda j, i: (0, 3 * nt + j)),
            pl.BlockSpec((SC_CONV, tc), lambda j, i: (0, j)),
        ],
        out_specs=pl.BlockSpec((tm, tc), lambda j, i: (i, j)),
        scratch_shapes=[pltpu.VMEM((4, dm, tc), BF16), pltpu.VMEM((tm, tc), F32), pltpu.VMEM((tm, tc), F32)],
        compiler_params=_params("arbitrary", "arbitrary"),
        name="sc_proj",
    )(hn, w, w, w, w, conv_w)


def _final_norm_kernel(x_ref, w_ref, o_ref):
    w = w_ref[...]

    def body(r, _):
        rows = pl.ds(pl.multiple_of(r * NORM_ROWS, NORM_ROWS), NORM_ROWS)
        x = x_ref[rows, :]
        ms = jnp.mean(x * x, axis=-1, keepdims=True)
        o_ref[rows, :] = x * lax.rsqrt(ms + EPS) * w
        return 0

    lax.fori_loop(0, x_ref.shape[0] // NORM_ROWS, body, 0, unroll=4)


def _final_norm(x2d, w, tm=256):
    m, dm = x2d.shape
    return pl.pallas_call(
        _final_norm_kernel,
        out_shape=jax.ShapeDtypeStruct((m, dm), F32),
        grid=(m // tm,),
        in_specs=[pl.BlockSpec((tm, dm), lambda i: (i, 0)), pl.BlockSpec((1, dm), lambda i: (0, 0))],
        out_specs=pl.BlockSpec((tm, dm), lambda i: (i, 0)),
        compiler_params=_params("arbitrary"),
        name="final_norm",
    )(x2d, w)


def _gate_lane_vector(p):
    z = jnp.zeros((DN_HEADS,), F32)
    cols = jnp.stack([z, z, p[0], p[1], p[0], p[1], z, z], axis=1)
    return cols.reshape(1, LANES).astype(F32)


def _gate_row_selection():
    p = np.zeros((LANES, LANES), np.float32)
    for h in range(DN_HEADS):
        for slot, src in enumerate([h, DN_HEADS + h, 2 * DN_HEADS + h, 3 * DN_HEADS + h, 2 * DN_HEADS + h, 3 * DN_HEADS + h]):
            p[h * GATE_SLOTS + slot, src] = 1.0
    return jnp.asarray(p, BF16)


def kernel(x, c, ctx, c_ctx, mod_w, mod_b, norm_w, ab_w_in, ab_qkv_conv, ab_a_log, ab_dt_bias, ab_dn_norm,
           ab_lru_conv_w, ab_lru_conv_b, ab_lru_w_r, ab_lru_b_r, ab_lru_w_i, ab_lru_b_i, ab_lru_lambda, ab_w_out,
           sc_w_in, sc_conv, sc_w_out, final_norm_w):
    dm = D_MODEL
    x2d = x.reshape(BATCH * SEQ, dm)
    ctx2d = ctx.reshape(BATCH * CTX_LEN, dm)

    c8 = jnp.concatenate([c, c_ctx[None, :], jnp.zeros((8 - BATCH - 1, dm), F32)], axis=0)
    mod = _modulation(c8, mod_w, mod_b)
    shift = [mod[l, :, :dm].reshape(8, 1, dm) for l in range(2)]
    scale = [mod[l, :, dm:2 * dm].reshape(8, 1, dm) for l in range(2)]
    gate = [mod[l, :, 2 * dm:].reshape(8, 1, dm) for l in range(2)]

    w_in_t = jnp.swapaxes(ab_w_in[0], 0, 1)
    nw0 = norm_w[0].reshape(1, dm)
    hn = _norm_mod(x2d, ctx2d, nw0, scale[0], shift[0])
    proj = _proj0(hn, w_in_t)
    ba = _gate_proj(hn, w_in_t)

    n_lat = BATCH * SEQ
    ba_all = jnp.concatenate([ba[n_lat:].reshape(BATCH, CTX_LEN, LANES), ba[:n_lat].reshape(BATCH, SEQ, LANES)], axis=1)
    gates_tok = _gates(ba_all.reshape(BATCH * ALL_LEN, LANES), _gate_lane_vector(ab_a_log[0]),
                       _gate_lane_vector(ab_dt_bias[0])).reshape(BATCH, ALL_LEN, LANES)
    gates_t = gates_tok.transpose(0, 2, 1)

    y_dn = _deltanet(proj, ab_qkv_conv[0], gates_tok, gates_t, ab_dn_norm[0].reshape(1, DN_HEAD_DIM))

    wcat = jnp.concatenate([ab_lru_w_r[0, 0], ab_lru_w_i[0, 0], ab_lru_w_r[0, 1], ab_lru_w_i[0, 1]],
                           axis=-1).astype(BF16)
    bl = LRU_BLOCK_DIM
    bcat = jnp.concatenate([ab_lru_b_r[0, 0].reshape(LRU_BLOCKS, 1, bl), ab_lru_b_i[0, 0].reshape(LRU_BLOCKS, 1, bl),
                            ab_lru_b_r[0, 1].reshape(LRU_BLOCKS, 1, bl), ab_lru_b_i[0, 1].reshape(LRU_BLOCKS, 1, bl)],
                           axis=-1)
    lam = ab_lru_lambda[0].reshape(2, LRU_BLOCKS, bl).transpose(1, 0, 2)
    y_lru = _lru(proj, ab_lru_conv_w[0], ab_lru_conv_b[0].reshape(1, LRU_WIDTH), wcat, bcat, lam)

    x1 = _out_proj(y_dn, y_lru, 0, 0, ab_w_out[0], x2d, gate[0])

    hn1 = _norm_mod(x1, None, norm_w[1].reshape(1, dm), scale[1], shift[1])
    y1 = _sc_proj(hn1, sc_w_in[0], sc_conv[0])
    x2 = _out_proj(y1, y1, 0, 1, sc_w_out[0], x1, gate[1])

    return _final_norm(x2, final_norm_w.reshape(1, dm)).reshape(BATCH, SEQ, dm)
```

```python
import functools

import jax
import jax.numpy as jnp
import numpy as np
from jax import lax
from jax.experimental import pallas as pl
from jax.experimental.pallas import tpu as pltpu

F32 = jnp.float32
BF16 = jnp.bfloat16

D_MODEL = 4096
BATCH = 2
SEQ = 4096
CTX_LEN = 256
GRID_W = 64
GRID_ROWS = SEQ // GRID_W
EPS = 1e-6

DN_HEADS = 16
DN_HEAD_DIM = 128
DN_WIDTH = DN_HEADS * DN_HEAD_DIM
DN_CONV = 4
DN_CHUNK = 64

LRU_WIDTH = 2048
LRU_BLOCKS = 16
LRU_BLOCK_DIM = LRU_WIDTH // LRU_BLOCKS
LRU_CONV = 4
LRU_C = 8.0

SC_WIDTH = D_MODEL
SC_CONV = 3

OFF_LRU = 3 * DN_WIDTH
OFF_BETA = OFF_LRU + LRU_WIDTH
OFF_ALPHA = OFF_BETA + 2 * DN_HEADS
AB_STATE = OFF_ALPHA + 2 * DN_HEADS

P_K = DN_WIDTH
P_V = 2 * DN_WIDTH
P_LRU = 3 * DN_WIDTH
P_DNG = P_LRU + LRU_WIDTH
P_LRG = P_DNG + DN_WIDTH
P_WIDTH = P_LRG + LRU_WIDTH
LANES = 128
GATE_SLOTS = LANES // DN_HEADS

N_CTX_CHUNKS = CTX_LEN // DN_CHUNK
N_SEQ_CHUNKS = SEQ // DN_CHUNK
N_SLOTS = N_CTX_CHUNKS + N_SEQ_CHUNKS
ALL_LEN = CTX_LEN + SEQ

LRU_SEGS = 8
NEG_BIG = -1e30

VMEM_LIMIT = 56 * 1024 * 1024


def _params(*sem):
    return pltpu.CompilerParams(dimension_semantics=sem, vmem_limit_bytes=VMEM_LIMIT)


def _mm(a, b):
    return jnp.dot(a, b, preferred_element_type=F32)


def _mo(v, m):
    return v if isinstance(v, int) else pl.multiple_of(v, m)


def _silu(x):
    return x * jax.nn.sigmoid(x)


def _sigmoid_tanh(x):
    return 0.5 * jnp.tanh(0.5 * x) + 0.5


def _softplus(x):
    return jnp.maximum(x, 0.0) + jnp.log1p(jnp.exp(-jnp.abs(x)))


def _mod_kernel(c_ref, w_ref, b_ref, o_ref):
    s = _silu(c_ref[...]).astype(BF16)
    o_ref[...] = _mm(s, w_ref[...].astype(BF16)) + b_ref[...]


def _modulation(c8, mod_w, mod_b):
    depth, dm, n = mod_w.shape
    tn = 512
    return pl.pallas_call(
        _mod_kernel,
        out_shape=jax.ShapeDtypeStruct((depth, 8, n), F32),
        grid=(depth, n // tn),
        in_specs=[
            pl.BlockSpec((8, dm), lambda l, j: (0, 0)),
            pl.BlockSpec((None, dm, tn), lambda l, j: (l, 0, j)),
            pl.BlockSpec((None, 1, tn), lambda l, j: (l, 0, j)),
        ],
        out_specs=pl.BlockSpec((None, 8, tn), lambda l, j: (l, 0, j)),
        compiler_params=_params("arbitrary", "arbitrary"),
        name="modulation",
    )(c8, mod_w, mod_b.reshape(depth, 1, n))


NORM_ROWS = 8


def _norm_rows(x_ref, nw_ref, sc_ref, sh_ref, hn_ref):
    tm = x_ref.shape[0]
    wsc = nw_ref[...] * (1.0 + sc_ref[...])
    sh = sh_ref[...]

    def body(r, _):
        rows = pl.ds(pl.multiple_of(r * NORM_ROWS, NORM_ROWS), NORM_ROWS)
        x = x_ref[rows, :]
        ms = jnp.mean(x * x, axis=-1, keepdims=True)
        hn_ref[rows, :] = (x * lax.rsqrt(ms + EPS) * wsc + sh).astype(BF16)
        return 0

    lax.fori_loop(0, tm // NORM_ROWS, body, 0, unroll=4)


def _norm_mod_kernel(x_ref, c_ref, nw_ref, sc_ref, sh_ref, o_ref, *, n_x_tiles):
    @pl.when(pl.program_id(0) < n_x_tiles)
    def _():
        _norm_rows(x_ref, nw_ref, sc_ref, sh_ref, o_ref)

    @pl.when(pl.program_id(0) >= n_x_tiles)
    def _():
        _norm_rows(c_ref, nw_ref, sc_ref, sh_ref, o_ref)


def _norm_mod(x2d, c2d, nw, sc3, sh3, tm=512):
    m, dm = x2d.shape
    mc = 0 if c2d is None else c2d.shape[0]
    c2d = x2d if c2d is None else c2d
    n_x_tiles = m // tm
    per_batch = SEQ // tm
    mod_spec = pl.BlockSpec((None, 1, dm), lambda i: (i // per_batch, 0, 0))
    return pl.pallas_call(
        functools.partial(_norm_mod_kernel, n_x_tiles=n_x_tiles),
        out_shape=jax.ShapeDtypeStruct((m + mc, dm), BF16),
        grid=(n_x_tiles + mc // tm,),
        in_specs=[
            pl.BlockSpec((tm, dm), lambda i: (jnp.minimum(i, n_x_tiles - 1), 0)),
            pl.BlockSpec((tm, dm), lambda i: (jnp.maximum(i - n_x_tiles, 0), 0)),
            pl.BlockSpec((1, dm), lambda i: (0, 0)),
            mod_spec,
            mod_spec,
        ],
        out_specs=pl.BlockSpec((tm, dm), lambda i: (i, 0)),
        compiler_params=_params("arbitrary"),
        name="norm_mod",
    )(x2d, c2d, nw, sc3, sh3)


W_CAST_ROWS = 64


def _proj0_kernel(a_ref, w_ref, wn_ref, o_ref, wb_ref, *, first_shifted_tile, shift):
    j = pl.program_id(0)
    tn, dm = w_ref.shape

    def cast_slabs(fn):
        def body(c, _):
            cols = pl.ds(pl.multiple_of(c * LANES, LANES), LANES)
            wb_ref[cols, :] = fn(cols).T.astype(BF16)
            return 0

        lax.fori_loop(0, dm // LANES, body, 0)

    @pl.when((pl.program_id(1) == 0) & (j < first_shifted_tile))
    def _():
        cast_slabs(lambda cols: w_ref[:, cols])

    @pl.when((pl.program_id(1) == 0) & (j >= first_shifted_tile))
    def _():
        cast_slabs(lambda cols: jnp.concatenate([w_ref[pl.ds(shift, tn - shift), cols], wn_ref[pl.ds(0, shift), cols]],
                                                axis=0))

    o_ref[...] = _mm(a_ref[...], wb_ref[...])


def _proj0(hn, w_in_t, tm=544, tn=1024):
    m, dm = hn.shape
    shift = AB_STATE - OFF_BETA
    return pl.pallas_call(
        functools.partial(_proj0_kernel, first_shifted_tile=OFF_BETA // tn, shift=shift),
        out_shape=jax.ShapeDtypeStruct((m, P_WIDTH), F32),
        grid=(P_WIDTH // tn, m // tm),
        in_specs=[
            pl.BlockSpec((tm, dm), lambda j, i: (i, 0)),
            pl.BlockSpec((tn, dm), lambda j, i: (j, 0), pipeline_mode=pl.Buffered(1)),
            pl.BlockSpec((shift, dm), lambda j, i: ((j + 1) * (tn // shift), 0)),
        ],
        out_specs=pl.BlockSpec((tm, tn), lambda j, i: (i, j)),
        scratch_shapes=[pltpu.VMEM((dm, tn), BF16)],
        compiler_params=_params("arbitrary", "arbitrary"),
        name="proj0",
    )(hn, w_in_t, w_in_t)


def _gate_proj_kernel(a_ref, w_ref, perm_ref, o_ref, wb_ref):
    @pl.when(pl.program_id(0) == 0)
    def _():
        wb_ref[...] = _mm(perm_ref[...], w_ref[...].astype(BF16)).astype(BF16)

    o_ref[...] = lax.dot_general(a_ref[...], wb_ref[...], (((1,), (1,)), ((), ())), preferred_element_type=F32)


def _gate_proj(a, w_in_t, tm=512):
    m, dm = a.shape
    return pl.pallas_call(
        _gate_proj_kernel,
        out_shape=jax.ShapeDtypeStruct((m, LANES), F32),
        grid=(m // tm,),
        in_specs=[
            pl.BlockSpec((tm, dm), lambda i: (i, 0)),
            pl.BlockSpec((LANES, dm), lambda i: (OFF_BETA // LANES, 0)),
            pl.BlockSpec((LANES, LANES), lambda i: (0, 0)),
        ],
        out_specs=pl.BlockSpec((tm, LANES), lambda i: (i, 0)),
        scratch_shapes=[pltpu.VMEM((LANES, dm), BF16)],
        compiler_params=_params("arbitrary"),
        name="gate_proj",
    )(a, w_in_t, _gate_row_selection())


def _split3(a):
    a1 = a.astype(BF16)
    r1 = a - a1.astype(F32)
    a2 = r1.astype(BF16)
    a3 = (r1 - a2.astype(F32)).astype(BF16)
    return a1, a2, a3


def _gates_kernel(ba_ref, alog_ref, dt_ref, o_ref):
    tm = ba_ref.shape[0]
    r = lax.broadcasted_iota(jnp.int32, (DN_CHUNK, DN_CHUNK), 0)
    c = lax.broadcasted_iota(jnp.int32, (DN_CHUNK, DN_CHUNK), 1)
    lower = (r >= c).astype(BF16)
    upper = (r <= c).astype(BF16)
    ones = jnp.ones((DN_CHUNK, DN_CHUNK), BF16)
    slot = lax.broadcasted_iota(jnp.int32, (DN_CHUNK, LANES), 1) % GATE_SLOTS
    neg_a = -jnp.exp(alog_ref[...])
    dt = dt_ref[...]

    def body(i, _):
        rows = pl.ds(pl.multiple_of(i * DN_CHUNK, DN_CHUNK), DN_CHUNK)
        x = ba_ref[rows, :]
        beta = jax.nn.sigmoid(x)
        g = neg_a * _softplus(x + dt)
        g1, g2, g3 = _split3(g)
        cum = _mm(lower, g1) + _mm(lower, g2) + _mm(lower, g3)
        suf = _mm(upper, g1) + _mm(upper, g2) + _mm(upper, g3)
        tot = _mm(ones, g1) + _mm(ones, g2) + _mm(ones, g3)
        o_ref[rows, :] = jnp.where(slot < 2, beta, jnp.where(slot == 2, cum, jnp.where(slot == 3, suf, tot)))
        return 0

    lax.fori_loop(0, tm // DN_CHUNK, body, 0)


def _gates(ba, alog_l, dt_l, tm=512):
    m = ba.shape[0]
    return pl.pallas_call(
        _gates_kernel,
        out_shape=jax.ShapeDtypeStruct((m, LANES), F32),
        grid=(m // tm,),
        in_specs=[
            pl.BlockSpec((tm, LANES), lambda i: (i, 0)),
            pl.BlockSpec((1, LANES), lambda i: (0, 0)),
            pl.BlockSpec((1, LANES), lambda i: (0, 0)),
        ],
        out_specs=pl.BlockSpec((tm, LANES), lambda i: (i, 0)),
        compiler_params=_params("arbitrary"),
        name="dn_gates",
    )(ba, alog_l, dt_l)


HALO = 8


def _conv4_rows(ref, w, start, length):
    n = DN_CHUNK
    lo, hi = (max, min) if isinstance(start, int) else (jnp.maximum, jnp.minimum)
    prev = ref[pl.ds(_mo(lo(start - HALO, 0), HALO), HALO), :]
    nxt = ref[pl.ds(_mo(hi(start + n, length - HALO), HALO), HALO), :]
    prev = jnp.where(start > 0, prev, 0.0)
    nxt = jnp.where(start + n < length, nxt, 0.0)
    v = jnp.concatenate([prev, ref[pl.ds(_mo(start, n), n), :], nxt], axis=0)
    rows = v.shape[0]
    acc = pltpu.roll(w[0:1] * v, 1, axis=0) + w[1:2] * v
    acc = pltpu.roll(acc, 1, axis=0) + w[2:3] * v + pltpu.roll(w[3:4] * v, rows - 1, axis=0)
    return acc[HALO:HALO + n]


DN_GROUP = 16
N_TRI_LEVELS = 6
M_STRICT, M_INCL, M_EYE, M_JOIN = 0, 1, 2, 3


def _tri_masks():
    n = DN_CHUNK
    r = np.arange(n)[:, None].repeat(2 * n, axis=1)
    c = np.arange(2 * n)[None, :].repeat(n, axis=0) % n
    upper = np.arange(2 * n)[None, :] >= n
    rr = np.where(upper, c, r)
    cc = np.where(upper, r, c)
    out = [rr > cc, rr >= cc, rr == cc]
    for lvl in range(N_TRI_LEVELS):
        s = 1 << lvl
        out.append((rr // (2 * s) == cc // (2 * s)) & (rr % (2 * s) >= s) & (cc % (2 * s) < s))
    return jnp.asarray(np.stack(out).astype(np.float32))


def _lane_halves(x):
    lane = lax.broadcasted_iota(jnp.int32, x.shape, 1)
    zero = jnp.zeros_like(x)
    return jnp.where(lane < DN_CHUNK, x, zero), jnp.where(lane >= DN_CHUNK, x, zero)


def _block_diag2(x):
    return jnp.concatenate(_lane_halves(x), axis=0)


def _packed_tri_inverse(a_list, m_ref):
    d_list = [m_ref[M_EYE] - m_ref[M_JOIN] * a for a in a_list]
    for lvl in range(1, N_TRI_LEVELS):
        join = m_ref[M_JOIN + lvl]
        l_bd = [_block_diag2((join * a).astype(BF16)) for a in a_list]
        d_b = [d.astype(BF16) for d in d_list]
        e_b = [_mm(db, lb).astype(BF16) for db, lb in zip(d_b, l_bd)]
        d_list = [d - _mm(eb, _block_diag2(db)) for d, eb, db in zip(d_list, e_b, d_b)]
    return d_list


def _dn_local_group(h, srcs, g_all, gt_all, slot0, m_ref, nq_ref, c_ref, ge_ref, acc_ref):
    n = DN_CHUNK
    lane = lax.broadcasted_iota(jnp.int32, (n, LANES), 1)
    lane1 = lax.broadcasted_iota(jnp.int32, (1, LANES), 1)
    upper = lane >= n
    strict = m_ref[M_STRICT]
    incl = m_ref[M_INCL] > 0.5
    nt = (((1,), (1,)), ((), ()))
    cols, a_list, decays, kqs = [], [], [], []
    for e, (kc, vc, qc) in enumerate(srcs):
        g = g_all[e * n:(e + 1) * n]

        def col(s, g=g):
            return jnp.sum(jnp.where(lane == h * GATE_SLOTS + s, g, 0.0), axis=1, keepdims=True)

        cs = [col(s) for s in range(6)]
        t0 = (e // 2) * LANES
        row_f = gt_all[2:3, t0:t0 + LANES]
        row_b = gt_all[3:4, t0:t0 + LANES]
        if e % 2 == 0:
            g_row = jnp.where(lane1 < n, row_f, pltpu.roll(row_b, n, axis=1))
        else:
            g_row = jnp.where(lane1 < n, pltpu.roll(row_f, n, axis=1), row_b)
        b_pack = jnp.where(upper, cs[1], cs[0])
        g_pack = jnp.where(upper, cs[3], cs[2])
        decay = jnp.exp(jnp.where(incl, g_pack - g_row, NEG_BIG))
        kb = kc.astype(BF16)
        lhs = kb if qc is None else jnp.concatenate([kb, qc.astype(BF16)], axis=0)
        kq = lax.dot_general(lhs, jnp.concatenate([kb, kb], axis=0), nt, preferred_element_type=F32)
        cols.append(cs)
        decays.append(decay)
        kqs.append(kq)
        a_list.append(strict * (b_pack * kq[:n] * decay))
    t_inv = _packed_tri_inverse(a_list, m_ref)
    for e, (kc, vc, qc) in enumerate(srcs):
        b_f, b_b, g_f, g_b, tot_f, tot_b = cols[e]
        eg = (jnp.exp(g_f), jnp.exp(g_b))
        rhs = jnp.concatenate([jnp.concatenate([kc * (b_f * eg[0]), vc * b_f], axis=1),
                               jnp.concatenate([kc * (b_b * eg[1]), vc * b_b], axis=1)], axis=0)
        wu = _mm(_block_diag2(t_inv[e].astype(BF16)), rhs.astype(BF16))
        slot = slot0 + e
        nq_rows = _mo(slot * 3 * n, n)
        ke_t = jnp.concatenate([kc * jnp.exp(tot_f - g_f), kc * jnp.exp(tot_b - g_b)], axis=0).T.astype(BF16)
        ke_f, ke_b = _lane_halves(ke_t)
        parts = [[ke_f], [ke_b]]
        if qc is not None:
            aq_f, aq_b = _lane_halves((kqs[e][n:] * decays[e]).astype(BF16))
            parts[0].append(aq_f)
            parts[1].append(aq_b)
        nc = _mm(jnp.concatenate(parts[0] + parts[1], axis=0), wu.astype(BF16))
        per_dir = nc.shape[0] // 2
        for d in range(2):
            tot_col = tot_f if d == 0 else tot_b
            ncd = nc[d * per_dir:(d + 1) * per_dir]
            ge_ref[d, pl.ds(_mo(slot * 8, 8), 8), :] = jnp.broadcast_to(jnp.exp(tot_col[:8]), (8, LANES))
            if qc is not None:
                nq_ref[d, pl.ds(nq_rows + 2 * n, n), :] = (qc * eg[d] - ncd[2 * n:, :DN_HEAD_DIM]).astype(BF16)
                rows = pl.ds(_mo((slot - N_CTX_CHUNKS) * n, n), n)
                acc_ref[rows, :] += ncd[2 * n:, DN_HEAD_DIM:]
            nq_ref[d, pl.ds(nq_rows, 2 * n), :] = ncd[:2 * n, :DN_HEAD_DIM].astype(BF16)
            c_ref[d, pl.ds(_mo(slot * 2 * n, 2 * n), 2 * n), :] = ncd[:2 * n, DN_HEAD_DIM:]


def _dn_state_step(d, slot, s, nq_ref, c_ref, ge_ref, with_output):
    n = DN_CHUNK
    nq_rows = _mo(slot * 3 * n, n)
    ge = jnp.tile(ge_ref[d, pl.ds(_mo(slot * 8, 8), 8), :], (DN_HEAD_DIM // 8, 1))
    c = c_ref[d, pl.ds(_mo(slot * 2 * n, 2 * n), 2 * n), :]
    r = _mm(nq_ref[d, pl.ds(nq_rows, (3 if with_output else 2) * n), :], s.astype(BF16))
    return s * ge - r[:2 * n] + c, (r[2 * n:] if with_output else None)


def _conv_silu_chunk(ref, w, start, length, l2_scale):
    a = _silu(_conv4_rows(ref, w, start, length))
    if l2_scale is None:
        return a
    return a * (lax.rsqrt(jnp.sum(a * a, axis=-1, keepdims=True) + EPS) * l2_scale)


def _deltanet_kernel(q_ref, k_ref, v_ref, kc_ref, vc_ref, wq_ref, wk_ref, wv_ref, g_ref, gt_ref, gate_ref, nw_ref,
                     m_ref, o_ref, nq_ref, c_ref, ge_ref, acc_ref):
    h = pl.program_id(1)
    n = DN_CHUNK
    gl = DN_GROUP * n
    acc_ref[...] = jnp.zeros_like(acc_ref)
    wq, wk, wv = wq_ref[...], wk_ref[...], wv_ref[...]

    srcs = [(_conv_silu_chunk(kc_ref, wk, e * n, CTX_LEN, 1.0), _conv_silu_chunk(vc_ref, wv, e * n, CTX_LEN, None), None)
            for e in range(N_CTX_CHUNKS)]
    _dn_local_group(h, srcs, g_ref[pl.ds(0, CTX_LEN), :], gt_ref[:, pl.ds(0, CTX_LEN)], 0,
                    m_ref, nq_ref, c_ref, ge_ref, acc_ref)

    def local_body(p, _):
        base = pl.multiple_of(p * gl, gl)
        gbase = pl.multiple_of(CTX_LEN + p * gl, gl)
        srcs = [(_conv_silu_chunk(k_ref, wk, base + e * n, SEQ, 1.0), _conv_silu_chunk(v_ref, wv, base + e * n, SEQ, None),
                 _conv_silu_chunk(q_ref, wq, base + e * n, SEQ, DN_HEAD_DIM ** -0.5)) for e in range(DN_GROUP)]
        _dn_local_group(h, srcs, g_ref[pl.ds(gbase, gl), :], gt_ref[:, pl.ds(gbase, gl)], N_CTX_CHUNKS + p * DN_GROUP,
                        m_ref, nq_ref, c_ref, ge_ref, acc_ref)
        return 0

    lax.fori_loop(0, N_SEQ_CHUNKS // DN_GROUP, local_body, 0)

    s_f = jnp.zeros((DN_HEAD_DIM, DN_HEAD_DIM), F32)
    s_b = jnp.zeros((DN_HEAD_DIM, DN_HEAD_DIM), F32)
    for i in range(N_CTX_CHUNKS):
        s_f, _ = _dn_state_step(0, i, s_f, nq_ref, c_ref, ge_ref, False)
        s_b, _ = _dn_state_step(1, N_CTX_CHUNKS - 1 - i, s_b, nq_ref, c_ref, ge_ref, False)

    def state_body(i, carry):
        s_f, s_b = carry
        s_f, o_f = _dn_state_step(0, N_CTX_CHUNKS + i, s_f, nq_ref, c_ref, ge_ref, True)
        rows_f = pl.ds(pl.multiple_of(i * n, n), n)
        acc_ref[rows_f, :] += o_f
        jb = N_SEQ_CHUNKS - 1 - i
        s_b, o_b = _dn_state_step(1, N_CTX_CHUNKS + jb, s_b, nq_ref, c_ref, ge_ref, True)
        rows_b = pl.ds(pl.multiple_of(jb * n, n), n)
        acc_ref[rows_b, :] += o_b
        return s_f, s_b

    lax.fori_loop(0, N_SEQ_CHUNKS, state_body, (s_f, s_b), unroll=2)

    nw = nw_ref[...]

    out_rows = 4 * n

    def out_body(i, _):
        rows = pl.ds(pl.multiple_of(i * out_rows, out_rows), out_rows)
        o = acc_ref[rows, :]
        ms = jnp.mean(o * o, axis=-1, keepdims=True)
        o_ref[rows, :] = (o * lax.rsqrt(ms + EPS) * nw * _silu(gate_ref[rows, :])).astype(BF16)
        return 0

    lax.fori_loop(0, SEQ // out_rows, out_body, 0)


def _deltanet(proj, qkv_conv, gates_tok, gates_t, dn_norm):
    n = DN_CHUNK
    hb = DN_WIDTH // DN_HEAD_DIM
    ctx_blk0 = BATCH * SEQ // CTX_LEN
    return pl.pallas_call(
        _deltanet_kernel,
        out_shape=jax.ShapeDtypeStruct((BATCH * SEQ, DN_WIDTH), BF16),
        grid=(BATCH, DN_HEADS),
        in_specs=[
            pl.BlockSpec((SEQ, DN_HEAD_DIM), lambda b, h: (b, h)),
            pl.BlockSpec((SEQ, DN_HEAD_DIM), lambda b, h: (b, hb + h)),
            pl.BlockSpec((SEQ, DN_HEAD_DIM), lambda b, h: (b, 2 * hb + h)),
            pl.BlockSpec((CTX_LEN, DN_HEAD_DIM), lambda b, h: (ctx_blk0 + b, hb + h)),
            pl.BlockSpec((CTX_LEN, DN_HEAD_DIM), lambda b, h: (ctx_blk0 + b, 2 * hb + h)),
            pl.BlockSpec((DN_CONV, DN_HEAD_DIM), lambda b, h: (0, h)),
            pl.BlockSpec((DN_CONV, DN_HEAD_DIM), lambda b, h: (0, hb + h)),
            pl.BlockSpec((DN_CONV, DN_HEAD_DIM), lambda b, h: (0, 2 * hb + h)),
            pl.BlockSpec((None, ALL_LEN, LANES), lambda b, h: (b, 0, 0)),
            pl.BlockSpec((None, GATE_SLOTS, ALL_LEN), lambda b, h: (b, h, 0)),
            pl.BlockSpec((SEQ, DN_HEAD_DIM), lambda b, h: (b, P_DNG // DN_HEAD_DIM + h)),
            pl.BlockSpec((1, DN_HEAD_DIM), lambda b, h: (0, 0)),
            pl.BlockSpec((M_JOIN + N_TRI_LEVELS, n, 2 * n), lambda b, h: (0, 0, 0)),
        ],
        out_specs=pl.BlockSpec((SEQ, DN_HEAD_DIM), lambda b, h: (b, h)),
        scratch_shapes=[
            pltpu.VMEM((2, N_SLOTS * 3 * n, DN_HEAD_DIM), BF16),
            pltpu.VMEM((2, N_SLOTS * 2 * n, DN_HEAD_DIM), F32),
            pltpu.VMEM((2, N_SLOTS * 8, LANES), F32),
            pltpu.VMEM((SEQ, DN_HEAD_DIM), F32),
        ],
        compiler_params=_params("arbitrary", "arbitrary"),
        name="deltanet",
    )(proj, proj, proj, proj, proj, qkv_conv, qkv_conv, qkv_conv, gates_tok, gates_t, proj, dn_norm, _tri_masks())


def _shift_down(y):
    row = lax.broadcasted_iota(jnp.int32, y.shape, 0)
    return jnp.where(row == 0, 0.0, pltpu.roll(y, 1, axis=0))


def _shift_up(y):
    row = lax.broadcasted_iota(jnp.int32, y.shape, 0)
    return jnp.where(row == y.shape[0] - 1, 0.0, pltpu.roll(y, y.shape[0] - 1, axis=0))


def _lru_gate_rows(xc, wcat, bcat, sp, a_ref, b_ref, rows):
    g = _mm(xc.astype(BF16), wcat) + bcat
    w = LRU_BLOCK_DIM
    for d in range(2):
        r = _sigmoid_tanh(g[:, (2 * d) * w:(2 * d + 1) * w])
        i = _sigmoid_tanh(g[:, (2 * d + 1) * w:(2 * d + 2) * w])
        a = jnp.exp(-LRU_C * r * sp[d:d + 1, :])
        a_ref[d, rows, :] = a
        b_ref[d, rows, :] = jnp.sqrt(1.0 - a * a) * (i * xc)


def _carry_chain(h_end, p_end, c0, reverse):
    row = lax.broadcasted_iota(jnp.int32, h_end.shape, 0)
    carries = jnp.zeros_like(h_end)
    c = c0
    order = range(LRU_SEGS - 1, -1, -1) if reverse else range(LRU_SEGS)
    for gseg in order:
        carries = jnp.where(row == gseg, c, carries)
        c = h_end[gseg:gseg + 1, :] + p_end[gseg:gseg + 1, :] * c
    return carries, c


def _lru_kernel(x_ref, xc_in_ref, gate_ref, cw_ref, cb_ref, wcat_ref, bcat_ref, lam_ref, o_ref,
                xc_ref, xcc_ref, a_ref, b_ref, ac_ref, bc_ref, hl_ref, pp_ref, hs_ref):
    cw = cw_ref[...]
    cb = cb_ref[...]
    wcat = wcat_ref[...]
    bcat = bcat_ref[...]
    sp = _softplus(-lam_ref[...])
    gw = GRID_W

    for i in range(CTX_LEN // DN_CHUNK):
        xcc_ref[pl.ds(i * DN_CHUNK, DN_CHUNK), :] = _conv4_rows(xc_in_ref, cw, i * DN_CHUNK, CTX_LEN) + cb
    _lru_gate_rows(xcc_ref[...], wcat, bcat, sp, ac_ref, bc_ref, pl.ds(0, CTX_LEN))

    csteps = CTX_LEN // LRU_SEGS
    zero = jnp.zeros((LRU_SEGS, LRU_BLOCK_DIM), F32)
    one = jnp.ones((LRU_SEGS, LRU_BLOCK_DIM), F32)

    def ctx_body(j, carry):
        hf, pf, hb, pb = carry
        rf = pl.ds(j, LRU_SEGS, stride=csteps)
        a = ac_ref[0, rf, :]
        hf = a * hf + bc_ref[0, rf, :]
        pf = a * pf
        rb = pl.ds(csteps - 1 - j, LRU_SEGS, stride=csteps)
        a = ac_ref[1, rb, :]
        hb = a * hb + bc_ref[1, rb, :]
        pb = a * pb
        return hf, pf, hb, pb

    hf, pf, hb, pb = lax.fori_loop(0, csteps, ctx_body, (zero, one, zero, one))
    zrow = jnp.zeros((1, LRU_BLOCK_DIM), F32)
    _, h0_f = _carry_chain(hf, pf, zrow, False)
    _, h0_b = _carry_chain(hb, pb, zrow, True)

    def slab(r):
        return x_ref[pl.ds(pl.multiple_of(r * gw, gw), gw), :]

    def conv_slab(r, xm2, xm1, x0, xp1):
        xc_ref[pl.ds(pl.multiple_of(r * gw, gw), gw), :] = (
            cb + cw[0:1] * xm2 + cw[1:2] * xm1 + cw[2:3] * x0 + cw[3:4] * xp1)

    last = GRID_ROWS - 1
    conv_slab(0, _shift_down(slab(last - 1)), _shift_down(slab(last)), slab(0), slab(1))
    conv_slab(1, _shift_down(slab(last)), slab(0), slab(1), slab(2))
    conv_slab(last, slab(last - 2), slab(last - 1), slab(last), _shift_up(slab(0)))

    def conv_body(r, _):
        conv_slab(r, slab(r - 2), slab(r - 1), slab(r), slab(r + 1))
        return 0

    lax.fori_loop(2, last, conv_body, 0)

    gate_rows = 256

    def gate_body(i, _):
        rows = pl.ds(pl.multiple_of(i * gate_rows, gate_rows), gate_rows)
        _lru_gate_rows(xc_ref[rows, :], wcat, bcat, sp, a_ref, b_ref, rows)
        return 0

    lax.fori_loop(0, SEQ // gate_rows, gate_body, 0)

    steps = SEQ // LRU_SEGS

    seg_stride = GRID_W // LRU_SEGS
    unroll = 8

    def seg_rows(wq, r):
        return pl.ds(r * GRID_W + wq, LRU_SEGS, stride=seg_stride)

    def scan_col(wq, carry):
        def scan_rows(rr, carry):
            hf, pf, hb, pb = carry
            for k in range(unroll):
                r = rr * unroll + k
                rf = seg_rows(wq, r)
                a = a_ref[0, rf, :]
                hf = a * hf + b_ref[0, rf, :]
                pf = a * pf
                hl_ref[0, wq * GRID_ROWS + r] = hf
                pp_ref[0, wq * GRID_ROWS + r] = pf
                wqb = seg_stride - 1 - wq
                rb = GRID_ROWS - 1 - r
                rbk = seg_rows(wqb, rb)
                a = a_ref[1, rbk, :]
                hb = a * hb + b_ref[1, rbk, :]
                pb = a * pb
                hl_ref[1, wqb * GRID_ROWS + rb] = hb
                pp_ref[1, wqb * GRID_ROWS + rb] = pb
            return hf, pf, hb, pb

        return lax.fori_loop(0, GRID_ROWS // unroll, scan_rows, carry)

    hf, pf, hb, pb = lax.fori_loop(0, seg_stride, scan_col, (zero, one, zero, one))
    c_f, _ = _carry_chain(hf, pf, h0_f, False)
    c_b, _ = _carry_chain(hb, pb, h0_b, True)

    def fix_col(wq, _):
        def fix_rows(rr, _):
            for k in range(unroll):
                r = rr * unroll + k
                j = wq * GRID_ROWS + r
                h = hl_ref[0, j] + pp_ref[0, j] * c_f + hl_ref[1, j] + pp_ref[1, j] * c_b
                hs_ref[seg_rows(wq, r), :] = h
            return 0

        return lax.fori_loop(0, GRID_ROWS // unroll, fix_rows, 0)

    lax.fori_loop(0, seg_stride, fix_col, 0)

    def out_body(i, _):
        rows = pl.ds(pl.multiple_of(i * gate_rows, gate_rows), gate_rows)
        o_ref[rows, :] = (hs_ref[rows, :] * _silu(gate_ref[rows, :])).astype(BF16)
        return 0

    lax.fori_loop(0, SEQ // gate_rows, out_body, 0)


def _lru(proj, conv_w, conv_b, wcat, bcat, lam):
    w = LRU_BLOCK_DIM
    steps = SEQ // LRU_SEGS
    ctx_blk0 = BATCH * SEQ // CTX_LEN
    return pl.pallas_call(
        _lru_kernel,
        out_shape=jax.ShapeDtypeStruct((BATCH * SEQ, LRU_WIDTH), BF16),
        grid=(BATCH, LRU_BLOCKS),
        in_specs=[
            pl.BlockSpec((SEQ, w), lambda b, n: (b, P_LRU // w + n)),
            pl.BlockSpec((CTX_LEN, w), lambda b, n: (ctx_blk0 + b, P_LRU // w + n)),
            pl.BlockSpec((SEQ, w), lambda b, n: (b, P_LRG // w + n)),
            pl.BlockSpec((LRU_CONV, w), lambda b, n: (0, n)),
            pl.BlockSpec((1, w), lambda b, n: (0, n)),
            pl.BlockSpec((None, w, 4 * w), lambda b, n: (n, 0, 0)),
            pl.BlockSpec((None, 1, 4 * w), lambda b, n: (n, 0, 0)),
            pl.BlockSpec((None, 2, w), lambda b, n: (n, 0, 0)),
        ],
        out_specs=pl.BlockSpec((SEQ, w), lambda b, n: (b, n)),
        scratch_shapes=[
            pltpu.VMEM((SEQ, w), F32),
            pltpu.VMEM((CTX_LEN, w), F32),
            pltpu.VMEM((2, SEQ, w), F32),
            pltpu.VMEM((2, SEQ, w), F32),
            pltpu.VMEM((2, CTX_LEN, w), F32),
            pltpu.VMEM((2, CTX_LEN, w), F32),
            pltpu.VMEM((2, steps, LRU_SEGS, w), F32),
            pltpu.VMEM((2, steps, LRU_SEGS, w), F32),
            pltpu.VMEM((SEQ, w), F32),
        ],
        compiler_params=_params("arbitrary", "arbitrary"),
        name="rglru",
    )(proj, proj, proj, conv_w, conv_b, wcat, bcat, lam)


def _out_proj_kernel(ya_ref, yb_ref, wa_ref, wb_ref, x_ref, g_ref, o_ref, wcast_ref):
    @pl.when(pl.program_id(1) == 0)
    def _():
        def cast_body(r, _):
            rows = pl.ds(pl.multiple_of(r * W_CAST_ROWS, W_CAST_ROWS), W_CAST_ROWS)
            wcast_ref[0, rows, :] = wa_ref[rows, :].astype(BF16)
            wcast_ref[1, rows, :] = wb_ref[rows, :].astype(BF16)
            return 0

        lax.fori_loop(0, wa_ref.shape[0] // W_CAST_ROWS, cast_body, 0)

    acc = _mm(ya_ref[...], wcast_ref[0]) + _mm(yb_ref[...], wcast_ref[1])
    o_ref[...] = x_ref[...] + g_ref[...] * acc


def _out_proj(ya, yb, ya_blk, yb_blk, w, x2d, gate3, tm=1024, tn=512):
    m, dm = x2d.shape
    kh = w.shape[0] // 2
    per_batch = SEQ // tm
    return pl.pallas_call(
        _out_proj_kernel,
        out_shape=jax.ShapeDtypeStruct((m, dm), F32),
        grid=(dm // tn, m // tm),
        in_specs=[
            pl.BlockSpec((tm, kh), lambda j, i: (i, ya_blk)),
            pl.BlockSpec((tm, kh), lambda j, i: (i, yb_blk)),
            pl.BlockSpec((kh, tn), lambda j, i: (0, j)),
            pl.BlockSpec((kh, tn), lambda j, i: (1, j)),
            pl.BlockSpec((tm, tn), lambda j, i: (i, j)),
            pl.BlockSpec((None, 1, tn), lambda j, i: (i // per_batch, 0, j)),
        ],
        out_specs=pl.BlockSpec((tm, tn), lambda j, i: (i, j)),
        scratch_shapes=[pltpu.VMEM((2, kh, tn), BF16)],
        compiler_params=_params("arbitrary", "arbitrary"),
        name="out_proj",
    )(ya, yb, w, w, x2d, gate3)


def _sc_proj_kernel(a_ref, wb_ref, wc_ref, wx_ref, wg_ref, cw_ref, o_ref, wcast_ref, z_ref, bg_ref):
    @pl.when(pl.program_id(1) == 0)
    def _():
        def cast_body(r, _):
            rows = pl.ds(pl.multiple_of(r * W_CAST_ROWS, W_CAST_ROWS), W_CAST_ROWS)
            for gi, w_ref in enumerate((wb_ref, wc_ref, wx_ref, wg_ref)):
                wcast_ref[gi, rows, :] = w_ref[rows, :].astype(BF16)
            return 0

        lax.fori_loop(0, wb_ref.shape[0] // W_CAST_ROWS, cast_body, 0)

    hn = a_ref[...]
    z_ref[...] = _mm(hn, wcast_ref[1]) * _mm(hn, wcast_ref[2])
    bg_ref[...] = _mm(hn, wcast_ref[0]) * _silu(_mm(hn, wcast_ref[3]))
    cw = cw_ref[...]
    gw = GRID_W

    def body(r, _):
        rows = pl.ds(pl.multiple_of(r * gw, gw), gw)
        z = z_ref[rows, :]
        zc = cw[0:1] * _shift_down(z) + cw[1:2] * z + cw[2:3] * _shift_up(z)
        o_ref[rows, :] = (bg_ref[rows, :] * zc).astype(BF16)
        return 0

    lax.fori_loop(0, a_ref.shape[0] // gw, body, 0)


def _sc_proj(hn, w, conv_w, tm=512, tc=256):
    m, dm = hn.shape
    nt = SC_WIDTH // tc
    return pl.pallas_call(
        _sc_proj_kernel,
        out_shape=jax.ShapeDtypeStruct((m, SC_WIDTH), BF16),
        grid=(nt, m // tm),
        in_specs=[
            pl.BlockSpec((tm, dm), lambda j, i: (i, 0)),
            pl.BlockSpec((dm, tc), lambda j, i: (0, j)),
            pl.BlockSpec((dm, tc), lambda j, i: (0, nt + j)),
            pl.BlockSpec((dm, tc), lambda j, i: (0, 2 * nt + j)),
            pl.BlockSpec((dm, tc), lambda j, i: (0, 3 * nt + j)),
            pl.BlockSpec((SC_CONV, tc), lambda j, i: (0, j)),
        ],
        out_specs=pl.BlockSpec((tm, tc), lambda j, i: (i, j)),
        scratch_shapes=[pltpu.VMEM((4, dm, tc), BF16), pltpu.VMEM((tm, tc), F32), pltpu.VMEM((tm, tc), F32)],
        compiler_params=_params("arbitrary", "arbitrary"),
        name="sc_proj",
    )(hn, w, w, w, w, conv_w)


def _final_norm_kernel(x_ref, w_ref, o_ref):
    w = w_ref[...]

    def body(r, _):
        rows = pl.ds(pl.multiple_of(r * NORM_ROWS, NORM_ROWS), NORM_ROWS)
        x = x_ref[rows, :]
        ms = jnp.mean(x * x, axis=-1, keepdims=True)
        o_ref[rows, :] = x * lax.rsqrt(ms + EPS) * w
        return 0

    lax.fori_loop(0, x_ref.shape[0] // NORM_ROWS, body, 0, unroll=4)


def _final_norm(x2d, w, tm=256):
    m, dm = x2d.shape
    return pl.pallas_call(
        _final_norm_kernel,
        out_shape=jax.ShapeDtypeStruct((m, dm), F32),
        grid=(m // tm,),
        in_specs=[pl.BlockSpec((tm, dm), lambda i: (i, 0)), pl.BlockSpec((1, dm), lambda i: (0, 0))],
        out_specs=pl.BlockSpec((tm, dm), lambda i: (i, 0)),
        compiler_params=_params("arbitrary"),
        name="final_norm",
    )(x2d, w)


def _gate_lane_vector(p):
    z = jnp.zeros((DN_HEADS,), F32)
    cols = jnp.stack([z, z, p[0], p[1], p[0], p[1], z, z], axis=1)
    return cols.reshape(1, LANES).astype(F32)


def _gate_row_selection():
    p = np.zeros((LANES, LANES), np.float32)
    for h in range(DN_HEADS):
        for slot, src in enumerate([h, DN_HEADS + h, 2 * DN_HEADS + h, 3 * DN_HEADS + h, 2 * DN_HEADS + h, 3 * DN_HEADS + h]):
            p[h * GATE_SLOTS + slot, src] = 1.0
    return jnp.asarray(p, BF16)


def kernel(x, c, ctx, c_ctx, mod_w, mod_b, norm_w, ab_w_in, ab_qkv_conv, ab_a_log, ab_dt_bias, ab_dn_norm,
           ab_lru_conv_w, ab_lru_conv_b, ab_lru_w_r, ab_lru_b_r, ab_lru_w_i, ab_lru_b_i, ab_lru_lambda, ab_w_out,
           sc_w_in, sc_conv, sc_w_out, final_norm_w):
    dm = D_MODEL
    x2d = x.reshape(BATCH * SEQ, dm)
    ctx2d = ctx.reshape(BATCH * CTX_LEN, dm)

    c8 = jnp.concatenate([c, c_ctx[None, :], jnp.zeros((8 - BATCH - 1, dm), F32)], axis=0)
    mod = _modulation(c8, mod_w, mod_b)
    shift = [mod[l, :, :dm].reshape(8, 1, dm) for l in range(2)]
    scale = [mod[l, :, dm:2 * dm].reshape(8, 1, dm) for l in range(2)]
    gate = [mod[l, :, 2 * dm:].reshape(8, 1, dm) for l in range(2)]

    w_in_t = jnp.swapaxes(ab_w_in[0], 0, 1)
    nw0 = norm_w[0].reshape(1, dm)
    hn = _norm_mod(x2d, ctx2d, nw0, scale[0], shift[0])
    proj = _proj0(hn, w_in_t)
    ba = _gate_proj(hn, w_in_t)

    n_lat = BATCH * SEQ
    ba_all = jnp.concatenate([ba[n_lat:].reshape(BATCH, CTX_LEN, LANES), ba[:n_lat].reshape(BATCH, SEQ, LANES)], axis=1)
    gates_tok = _gates(ba_all.reshape(BATCH * ALL_LEN, LANES), _gate_lane_vector(ab_a_log[0]),
                       _gate_lane_vector(ab_dt_bias[0])).reshape(BATCH, ALL_LEN, LANES)
    gates_t = gates_tok.transpose(0, 2, 1)

    y_dn = _deltanet(proj, ab_qkv_conv[0], gates_tok, gates_t, ab_dn_norm[0].reshape(1, DN_HEAD_DIM))

    wcat = jnp.concatenate([ab_lru_w_r[0, 0], ab_lru_w_i[0, 0], ab_lru_w_r[0, 1], ab_lru_w_i[0, 1]],
                           axis=-1).astype(BF16)
    bl = LRU_BLOCK_DIM
    bcat = jnp.concatenate([ab_lru_b_r[0, 0].reshape(LRU_BLOCKS, 1, bl), ab_lru_b_i[0, 0].reshape(LRU_BLOCKS, 1, bl),
                            ab_lru_b_r[0, 1].reshape(LRU_BLOCKS, 1, bl), ab_lru_b_i[0, 1].reshape(LRU_BLOCKS, 1, bl)],
                           axis=-1)
    lam = ab_lru_lambda[0].reshape(2, LRU_BLOCKS, bl).transpose(1, 0, 2)
    y_lru = _lru(proj, ab_lru_conv_w[0], ab_lru_conv_b[0].reshape(1, LRU_WIDTH), wcat, bcat, lam)

    x1 = _out_proj(y_dn, y_lru, 0, 0, ab_w_out[0], x2d, gate[0])

    hn1 = _norm_mod(x1, None, norm_w[1].reshape(1, dm), scale[1], shift[1])
    y1 = _sc_proj(hn1, sc_w_in[0], sc_conv[0])
    x2 = _out_proj(y1, y1, 0, 1, sc_w_out[0], x1, gate[1])

    return _final_norm(x2, final_norm_w.reshape(1, dm)).reshape(BATCH, SEQ, dm)
```

```python
import functools

import jax
import jax.numpy as jnp
import numpy as np
from jax import lax
from jax.experimental import pallas as pl
from jax.experimental.pallas import tpu as pltpu

F32 = jnp.float32
BF16 = jnp.bfloat16

D_MODEL = 4096
BATCH = 2
SEQ = 4096
CTX_LEN = 256
GRID_W = 64
GRID_ROWS = SEQ // GRID_W
EPS = 1e-6

DN_HEADS = 16
DN_HEAD_DIM = 128
DN_WIDTH = DN_HEADS * DN_HEAD_DIM
DN_CONV = 4
DN_CHUNK = 64

LRU_WIDTH = 2048
LRU_BLOCKS = 16
LRU_BLOCK_DIM = LRU_WIDTH // LRU_BLOCKS
LRU_CONV = 4
LRU_C = 8.0

SC_WIDTH = D_MODEL
SC_CONV = 3

OFF_LRU = 3 * DN_WIDTH
OFF_BETA = OFF_LRU + LRU_WIDTH
OFF_ALPHA = OFF_BETA + 2 * DN_HEADS
AB_STATE = OFF_ALPHA + 2 * DN_HEADS

P_K = DN_WIDTH
P_V = 2 * DN_WIDTH
P_LRU = 3 * DN_WIDTH
P_DNG = P_LRU + LRU_WIDTH
P_LRG = P_DNG + DN_WIDTH
P_WIDTH = P_LRG + LRU_WIDTH
LANES = 128
GATE_SLOTS = LANES // DN_HEADS

N_CTX_CHUNKS = CTX_LEN // DN_CHUNK
N_SEQ_CHUNKS = SEQ // DN_CHUNK
N_SLOTS = N_CTX_CHUNKS + N_SEQ_CHUNKS
ALL_LEN = CTX_LEN + SEQ

LRU_SEGS = 8
NEG_BIG = -1e30

VMEM_LIMIT = 56 * 1024 * 1024


def _params(*sem):
    return pltpu.CompilerParams(dimension_semantics=sem, vmem_limit_bytes=VMEM_LIMIT)


def _mm(a, b):
    return jnp.dot(a, b, preferred_element_type=F32)


def _mo(v, m):
    return v if isinstance(v, int) else pl.multiple_of(v, m)


def _silu(x):
    return x * jax.nn.sigmoid(x)


def _sigmoid_tanh(x):
    return 0.5 * jnp.tanh(0.5 * x) + 0.5


def _softplus(x):
    return jnp.maximum(x, 0.0) + jnp.log1p(jnp.exp(-jnp.abs(x)))


def _mod_kernel(c_ref, w_ref, b_ref, o_ref):
    s = _silu(c_ref[...]).astype(BF16)
    o_ref[...] = _mm(s, w_ref[...].astype(BF16)) + b_ref[...]


def _modulation(c8, mod_w, mod_b):
    depth, dm, n = mod_w.shape
    tn = 512
    return pl.pallas_call(
        _mod_kernel,
        out_shape=jax.ShapeDtypeStruct((depth, 8, n), F32),
        grid=(depth, n // tn),
        in_specs=[
            pl.BlockSpec((8, dm), lambda l, j: (0, 0)),
            pl.BlockSpec((None, dm, tn), lambda l, j: (l, 0, j)),
            pl.BlockSpec((None, 1, tn), lambda l, j: (l, 0, j)),
        ],
        out_specs=pl.BlockSpec((None, 8, tn), lambda l, j: (l, 0, j)),
        compiler_params=_params("arbitrary", "arbitrary"),
        name="modulation",
    )(c8, mod_w, mod_b.reshape(depth, 1, n))


NORM_ROWS = 8


def _norm_rows(x_ref, nw_ref, sc_ref, sh_ref, hn_ref):
    tm = x_ref.shape[0]
    wsc = nw_ref[...] * (1.0 + sc_ref[...])
    sh = sh_ref[...]

    def body(r, _):
        rows = pl.ds(pl.multiple_of(r * NORM_ROWS, NORM_ROWS), NORM_ROWS)
        x = x_ref[rows, :]
        ms = jnp.mean(x * x, axis=-1, keepdims=True)
        hn_ref[rows, :] = (x * lax.rsqrt(ms + EPS) * wsc + sh).astype(BF16)
        return 0

    lax.fori_loop(0, tm // NORM_ROWS, body, 0, unroll=4)


def _norm_mod_kernel(x_ref, c_ref, nw_ref, sc_ref, sh_ref, o_ref, *, n_x_tiles):
    @pl.when(pl.program_id(0) < n_x_tiles)
    def _():
        _norm_rows(x_ref, nw_ref, sc_ref, sh_ref, o_ref)

    @pl.when(pl.program_id(0) >= n_x_tiles)
    def _():
        _norm_rows(c_ref, nw_ref, sc_ref, sh_ref, o_ref)


def _norm_mod(x2d, c2d, nw, sc3, sh3, tm=512):
    m, dm = x2d.shape
    mc = 0 if c2d is None else c2d.shape[0]
    c2d = x2d if c2d is None else c2d
    n_x_tiles = m // tm
    per_batch = SEQ // tm
    mod_spec = pl.BlockSpec((None, 1, dm), lambda i: (i // per_batch, 0, 0))
    return pl.pallas_call(
        functools.partial(_norm_mod_kernel, n_x_tiles=n_x_tiles),
        out_shape=jax.ShapeDtypeStruct((m + mc, dm), BF16),
        grid=(n_x_tiles + mc // tm,),
        in_specs=[
            pl.BlockSpec((tm, dm), lambda i: (jnp.minimum(i, n_x_tiles - 1), 0)),
            pl.BlockSpec((tm, dm), lambda i: (jnp.maximum(i - n_x_tiles, 0), 0)),
            pl.BlockSpec((1, dm), lambda i: (0, 0)),
            mod_spec,
            mod_spec,
        ],
        out_specs=pl.BlockSpec((tm, dm), lambda i: (i, 0)),
        compiler_params=_params("arbitrary"),
        name="norm_mod",
    )(x2d, c2d, nw, sc3, sh3)


W_CAST_ROWS = 64


def _proj0_kernel(a_ref, w_ref, wn_ref, o_ref, wb_ref, *, first_shifted_tile, shift):
    j = pl.program_id(0)
    tn, dm = w_ref.shape

    def cast_slabs(fn):
        def body(c, _):
            cols = pl.ds(pl.multiple_of(c * LANES, LANES), LANES)
            wb_ref[cols, :] = fn(cols).T.astype(BF16)
            return 0

        lax.fori_loop(0, dm // LANES, body, 0)

    @pl.when((pl.program_id(1) == 0) & (j < first_shifted_tile))
    def _():
        cast_slabs(lambda cols: w_ref[:, cols])

    @pl.when((pl.program_id(1) == 0) & (j >= first_shifted_tile))
    def _():
        cast_slabs(lambda cols: jnp.concatenate([w_ref[pl.ds(shift, tn - shift), cols], wn_ref[pl.ds(0, shift), cols]],
                                                axis=0))

    o_ref[...] = _mm(a_ref[...], wb_ref[...])


def _proj0(hn, w_in_t, tm=544, tn=1024):
    m, dm = hn.shape
    shift = AB_STATE - OFF_BETA
    return pl.pallas_call(
        functools.partial(_proj0_kernel, first_shifted_tile=OFF_BETA // tn, shift=shift),
        out_shape=jax.ShapeDtypeStruct((m, P_WIDTH), F32),
        grid=(P_WIDTH // tn, m // tm),
        in_specs=[
            pl.BlockSpec((tm, dm), lambda j, i: (i, 0)),
            pl.BlockSpec((tn, dm), lambda j, i: (j, 0), pipeline_mode=pl.Buffered(1)),
            pl.BlockSpec((shift, dm), lambda j, i: ((j + 1) * (tn // shift), 0)),
        ],
        out_specs=pl.BlockSpec((tm, tn), lambda j, i: (i, j)),
        scratch_shapes=[pltpu.VMEM((dm, tn), BF16)],
        compiler_params=_params("arbitrary", "arbitrary"),
        name="proj0",
    )(hn, w_in_t, w_in_t)


def _gate_proj_kernel(a_ref, w_ref, perm_ref, o_ref, wb_ref):
    @pl.when(pl.program_id(0) == 0)
    def _():
        wb_ref[...] = _mm(perm_ref[...], w_ref[...].astype(BF16)).astype(BF16)

    o_ref[...] = lax.dot_general(a_ref[...], wb_ref[...], (((1,), (1,)), ((), ())), preferred_element_type=F32)


def _gate_proj(a, w_in_t, tm=512):
    m, dm = a.shape
    return pl.pallas_call(
        _gate_proj_kernel,
        out_shape=jax.ShapeDtypeStruct((m, LANES), F32),
        grid=(m // tm,),
        in_specs=[
            pl.BlockSpec((tm, dm), lambda i: (i, 0)),
            pl.BlockSpec((LANES, dm), lambda i: (OFF_BETA // LANES, 0)),
            pl.BlockSpec((LANES, LANES), lambda i: (0, 0)),
        ],
        out_specs=pl.BlockSpec((tm, LANES), lambda i: (i, 0)),
        scratch_shapes=[pltpu.VMEM((LANES, dm), BF16)],
        compiler_params=_params("arbitrary"),
        name="gate_proj",
    )(a, w_in_t, _gate_row_selection())


def _split3(a):
    a1 = a.astype(BF16)
    r1 = a - a1.astype(F32)
    a2 = r1.astype(BF16)
    a3 = (r1 - a2.astype(F32)).astype(BF16)
    return a1, a2, a3


def _gates_kernel(ba_ref, alog_ref, dt_ref, o_ref):
    tm = ba_ref.shape[0]
    r = lax.broadcasted_iota(jnp.int32, (DN_CHUNK, DN_CHUNK), 0)
    c = lax.broadcasted_iota(jnp.int32, (DN_CHUNK, DN_CHUNK), 1)
    lower = (r >= c).astype(BF16)
    upper = (r <= c).astype(BF16)
    ones = jnp.ones((DN_CHUNK, DN_CHUNK), BF16)
    slot = lax.broadcasted_iota(jnp.int32, (DN_CHUNK, LANES), 1) % GATE_SLOTS
    neg_a = -jnp.exp(alog_ref[...])
    dt = dt_ref[...]

    def body(i, _):
        rows = pl.ds(pl.multiple_of(i * DN_CHUNK, DN_CHUNK), DN_CHUNK)
        x = ba_ref[rows, :]
        beta = jax.nn.sigmoid(x)
        g = neg_a * _softplus(x + dt)
        g1, g2, g3 = _split3(g)
        cum = _mm(lower, g1) + _mm(lower, g2) + _mm(lower, g3)
        suf = _mm(upper, g1) + _mm(upper, g2) + _mm(upper, g3)
        tot = _mm(ones, g1) + _mm(ones, g2) + _mm(ones, g3)
        o_ref[rows, :] = jnp.where(slot < 2, beta, jnp.where(slot == 2, cum, jnp.where(slot == 3, suf, tot)))
        return 0

    lax.fori_loop(0, tm // DN_CHUNK, body, 0)


def _gates(ba, alog_l, dt_l, tm=512):
    m = ba.shape[0]
    return pl.pallas_call(
        _gates_kernel,
        out_shape=jax.ShapeDtypeStruct((m, LANES), F32),
        grid=(m // tm,),
        in_specs=[
            pl.BlockSpec((tm, LANES), lambda i: (i, 0)),
            pl.BlockSpec((1, LANES), lambda i: (0, 0)),
            pl.BlockSpec((1, LANES), lambda i: (0, 0)),
        ],
        out_specs=pl.BlockSpec((tm, LANES), lambda i: (i, 0)),
        compiler_params=_params("arbitrary"),
        name="dn_gates",
    )(ba, alog_l, dt_l)


HALO = 8


def _conv4_rows(ref, w, start, length):
    n = DN_CHUNK
    lo, hi = (max, min) if isinstance(start, int) else (jnp.maximum, jnp.minimum)
    prev = ref[pl.ds(_mo(lo(start - HALO, 0), HALO), HALO), :]
    nxt = ref[pl.ds(_mo(hi(start + n, length - HALO), HALO), HALO), :]
    prev = jnp.where(start > 0, prev, 0.0)
    nxt = jnp.where(start + n < length, nxt, 0.0)
    v = jnp.concatenate([prev, ref[pl.ds(_mo(start, n), n), :], nxt], axis=0)
    rows = v.shape[0]
    acc = pltpu.roll(w[0:1] * v, 1, axis=0) + w[1:2] * v
    acc = pltpu.roll(acc, 1, axis=0) + w[2:3] * v + pltpu.roll(w[3:4] * v, rows - 1, axis=0)
    return acc[HALO:HALO + n]


DN_GROUP = 16
N_TRI_LEVELS = 6
M_STRICT, M_INCL, M_EYE, M_JOIN = 0, 1, 2, 3


def _tri_masks():
    n = DN_CHUNK
    r = np.arange(n)[:, None].repeat(2 * n, axis=1)
    c = np.arange(2 * n)[None, :].repeat(n, axis=0) % n
    upper = np.arange(2 * n)[None, :] >= n
    rr = np.where(upper, c, r)
    cc = np.where(upper, r, c)
    out = [rr > cc, rr >= cc, rr == cc]
    for lvl in range(N_TRI_LEVELS):
        s = 1 << lvl
        out.append((rr // (2 * s) == cc // (2 * s)) & (rr % (2 * s) >= s) & (cc % (2 * s) < s))
    return jnp.asarray(np.stack(out).astype(np.float32))


def _lane_halves(x):
    lane = lax.broadcasted_iota(jnp.int32, x.shape, 1)
    zero = jnp.zeros_like(x)
    return jnp.where(lane < DN_CHUNK, x, zero), jnp.where(lane >= DN_CHUNK, x, zero)


def _block_diag2(x):
    return jnp.concatenate(_lane_halves(x), axis=0)


def _packed_tri_inverse(a_list, m_ref):
    d_list = [m_ref[M_EYE] - m_ref[M_JOIN] * a for a in a_list]
    for lvl in range(1, N_TRI_LEVELS):
        join = m_ref[M_JOIN + lvl]
        l_bd = [_block_diag2((join * a).astype(BF16)) for a in a_list]
        d_b = [d.astype(BF16) for d in d_list]
        e_b = [_mm(db, lb).astype(BF16) for db, lb in zip(d_b, l_bd)]
        d_list = [d - _mm(eb, _block_diag2(db)) for d, eb, db in zip(d_list, e_b, d_b)]
    return d_list


def _dn_local_group(h, srcs, g_all, gt_all, slot0, m_ref, nq_ref, c_ref, ge_ref, acc_ref):
    n = DN_CHUNK
    lane = lax.broadcasted_iota(jnp.int32, (n, LANES), 1)
    lane1 = lax.broadcasted_iota(jnp.int32, (1, LANES), 1)
    upper = lane >= n
    strict = m_ref[M_STRICT]
    incl = m_ref[M_INCL] > 0.5
    nt = (((1,), (1,)), ((), ()))
    cols, a_list, decays, kqs = [], [], [], []
    for e, (kc, vc, qc) in enumerate(srcs):
        g = g_all[e * n:(e + 1) * n]

        def col(s, g=g):
            return jnp.sum(jnp.where(lane == h * GATE_SLOTS + s, g, 0.0), axis=1, keepdims=True)

        cs = [col(s) for s in range(6)]
        t0 = (e // 2) * LANES
        row_f = gt_all[2:3, t0:t0 + LANES]
        row_b = gt_all[3:4, t0:t0 + LANES]
        if e % 2 == 0:
            g_row = jnp.where(lane1 < n, row_f, pltpu.roll(row_b, n, axis=1))
        else:
            g_row = jnp.where(lane1 < n, pltpu.roll(row_f, n, axis=1), row_b)
        b_pack = jnp.where(upper, cs[1], cs[0])
        g_pack = jnp.where(upper, cs[3], cs[2])
        decay = jnp.exp(jnp.where(incl, g_pack - g_row, NEG_BIG))
        kb = kc.astype(BF16)
        lhs = kb if qc is None else jnp.concatenate([kb, qc.astype(BF16)], axis=0)
        kq = lax.dot_general(lhs, jnp.concatenate([kb, kb], axis=0), nt, preferred_element_type=F32)
        cols.append(cs)
        decays.append(decay)
        kqs.append(kq)
        a_list.append(strict * (b_pack * kq[:n] * decay))
    t_inv = _packed_tri_inverse(a_list, m_ref)
    for e, (kc, vc, qc) in enumerate(srcs):
        b_f, b_b, g_f, g_b, tot_f, tot_b = cols[e]
        eg = (jnp.exp(g_f), jnp.exp(g_b))
        rhs = jnp.concatenate([jnp.concatenate([kc * (b_f * eg[0]), vc * b_f], axis=1),
                               jnp.concatenate([kc * (b_b * eg[1]), vc * b_b], axis=1)], axis=0)
        wu = _mm(_block_diag2(t_inv[e].astype(BF16)), rhs.astype(BF16))
        slot = slot0 + e
        nq_rows = _mo(slot * 3 * n, n)
        ke_t = jnp.concatenate([kc * jnp.exp(tot_f - g_f), kc * jnp.exp(tot_b - g_b)], axis=0).T.astype(BF16)
        ke_f, ke_b = _lane_halves(ke_t)
        parts = [[ke_f], [ke_b]]
        if qc is not None:
            aq_f, aq_b = _lane_halves((kqs[e][n:] * decays[e]).astype(BF16))
            parts[0].append(aq_f)
            parts[1].append(aq_b)
        nc = _mm(jnp.concatenate(parts[0] + parts[1], axis=0), wu.astype(BF16))
        per_dir = nc.shape[0] // 2
        for d in range(2):
            tot_col = tot_f if d == 0 else tot_b
            ncd = nc[d * per_dir:(d + 1) * per_dir]
            ge_ref[d, pl.ds(_mo(slot * 8, 8), 8), :] = jnp.broadcast_to(jnp.exp(tot_col[:8]), (8, LANES))
            if qc is not None:
                nq_ref[d, pl.ds(nq_rows + 2 * n, n), :] = (qc * eg[d] - ncd[2 * n:, :DN_HEAD_DIM]).astype(BF16)
                rows = pl.ds(_mo((slot - N_CTX_CHUNKS) * n, n), n)
                acc_ref[rows, :] += ncd[2 * n:, DN_HEAD_DIM:]
            nq_ref[d, pl.ds(nq_rows, 2 * n), :] = ncd[:2 * n, :DN_HEAD_DIM].astype(BF16)
            c_ref[d, pl.ds(_mo(slot * 2 * n, 2 * n), 2 * n), :] = ncd[:2 * n, DN_HEAD_DIM:]


def _dn_state_step(d, slot, s, nq_ref, c_ref, ge_ref, with_output):
    n = DN_CHUNK
    nq_rows = _mo(slot * 3 * n, n)
    ge = jnp.tile(ge_ref[d, pl.ds(_mo(slot * 8, 8), 8), :], (DN_HEAD_DIM // 8, 1))
    c = c_ref[d, pl.ds(_mo(slot * 2 * n, 2 * n), 2 * n), :]
    r = _mm(nq_ref[d, pl.ds(nq_rows, (3 if with_output else 2) * n), :], s.astype(BF16))
    return s * ge - r[:2 * n] + c, (r[2 * n:] if with_output else None)


def _conv_silu_chunk(ref, w, start, length, l2_scale):
    a = _silu(_conv4_rows(ref, w, start, length))
    if l2_scale is None:
        return a
    return a * (lax.rsqrt(jnp.sum(a * a, axis=-1, keepdims=True) + EPS) * l2_scale)


def _deltanet_kernel(q_ref, k_ref, v_ref, kc_ref, vc_ref, wq_ref, wk_ref, wv_ref, g_ref, gt_ref, gate_ref, nw_ref,
                     m_ref, o_ref, nq_ref, c_ref, ge_ref, acc_ref):
    h = pl.program_id(1)
    n = DN_CHUNK
    gl = DN_GROUP * n
    acc_ref[...] = jnp.zeros_like(acc_ref)
    wq, wk, wv = wq_ref[...], wk_ref[...], wv_ref[...]

    srcs = [(_conv_silu_chunk(kc_ref, wk, e * n, CTX_LEN, 1.0), _conv_silu_chunk(vc_ref, wv, e * n, CTX_LEN, None), None)
            for e in range(N_CTX_CHUNKS)]
    _dn_local_group(h, srcs, g_ref[pl.ds(0, CTX_LEN), :], gt_ref[:, pl.ds(0, CTX_LEN)], 0,
                    m_ref, nq_ref, c_ref, ge_ref, acc_ref)

    def local_body(p, _):
        base = pl.multiple_of(p * gl, gl)
        gbase = pl.multiple_of(CTX_LEN + p * gl, gl)
        srcs = [(_conv_silu_chunk(k_ref, wk, base + e * n, SEQ, 1.0), _conv_silu_chunk(v_ref, wv, base + e * n, SEQ, None),
                 _conv_silu_chunk(q_ref, wq, base + e * n, SEQ, DN_HEAD_DIM ** -0.5)) for e in range(DN_GROUP)]
        _dn_local_group(h, srcs, g_ref[pl.ds(gbase, gl), :], gt_ref[:, pl.ds(gbase, gl)], N_CTX_CHUNKS + p * DN_GROUP,
                        m_ref, nq_ref, c_ref, ge_ref, acc_ref)
        return 0

    lax.fori_loop(0, N_SEQ_CHUNKS // DN_GROUP, local_body, 0)

    s_f = jnp.zeros((DN_HEAD_DIM, DN_HEAD_DIM), F32)
    s_b = jnp.zeros((DN_HEAD_DIM, DN_HEAD_DIM), F32)
    for i in range(N_CTX_CHUNKS):
        s_f, _ = _dn_state_step(0, i, s_f, nq_ref, c_ref, ge_ref, False)
        s_b, _ = _dn_state_step(1, N_CTX_CHUNKS - 1 - i, s_b, nq_ref, c_ref, ge_ref, False)

    def state_body(i, carry):
        s_f, s_b = carry
        s_f, o_f = _dn_state_step(0, N_CTX_CHUNKS + i, s_f, nq_ref, c_ref, ge_ref, True)
        rows_f = pl.ds(pl.multiple_of(i * n, n), n)
        acc_ref[rows_f, :] += o_f
        jb = N_SEQ_CHUNKS - 1 - i
        s_b, o_b = _dn_state_step(1, N_CTX_CHUNKS + jb, s_b, nq_ref, c_ref, ge_ref, True)
        rows_b = pl.ds(pl.multiple_of(jb * n, n), n)
        acc_ref[rows_b, :] += o_b
        return s_f, s_b

    lax.fori_loop(0, N_SEQ_CHUNKS, state_body, (s_f, s_b), unroll=4)

    nw = nw_ref[...]

    out_rows = 8 * n

    def out_body(i, _):
        rows = pl.ds(pl.multiple_of(i * out_rows, out_rows), out_rows)
        o = acc_ref[rows, :]
        ms = jnp.mean(o * o, axis=-1, keepdims=True)
        o_ref[rows, :] = (o * lax.rsqrt(ms + EPS) * nw * _silu(gate_ref[rows, :])).astype(BF16)
        return 0

    lax.fori_loop(0, SEQ // out_rows, out_body, 0)


def _deltanet(proj, qkv_conv, gates_tok, gates_t, dn_norm):
    n = DN_CHUNK
    hb = DN_WIDTH // DN_HEAD_DIM
    ctx_blk0 = BATCH * SEQ // CTX_LEN
    return pl.pallas_call(
        _deltanet_kernel,
        out_shape=jax.ShapeDtypeStruct((BATCH * SEQ, DN_WIDTH), BF16),
        grid=(BATCH, DN_HEADS),
        in_specs=[
            pl.BlockSpec((SEQ, DN_HEAD_DIM), lambda b, h: (b, h)),
            pl.BlockSpec((SEQ, DN_HEAD_DIM), lambda b, h: (b, hb + h)),
            pl.BlockSpec((SEQ, DN_HEAD_DIM), lambda b, h: (b, 2 * hb + h)),
            pl.BlockSpec((CTX_LEN, DN_HEAD_DIM), lambda b, h: (ctx_blk0 + b, hb + h)),
            pl.BlockSpec((CTX_LEN, DN_HEAD_DIM), lambda b, h: (ctx_blk0 + b, 2 * hb + h)),
            pl.BlockSpec((DN_CONV, DN_HEAD_DIM), lambda b, h: (0, h)),
            pl.BlockSpec((DN_CONV, DN_HEAD_DIM), lambda b, h: (0, hb + h)),
            pl.BlockSpec((DN_CONV, DN_HEAD_DIM), lambda b, h: (0, 2 * hb + h)),
            pl.BlockSpec((None, ALL_LEN, LANES), lambda b, h: (b, 0, 0)),
            pl.BlockSpec((None, GATE_SLOTS, ALL_LEN), lambda b, h: (b, h, 0)),
            pl.BlockSpec((SEQ, DN_HEAD_DIM), lambda b, h: (b, P_DNG // DN_HEAD_DIM + h)),
            pl.BlockSpec((1, DN_HEAD_DIM), lambda b, h: (0, 0)),
            pl.BlockSpec((M_JOIN + N_TRI_LEVELS, n, 2 * n), lambda b, h: (0, 0, 0)),
        ],
        out_specs=pl.BlockSpec((SEQ, DN_HEAD_DIM), lambda b, h: (b, h)),
        scratch_shapes=[
            pltpu.VMEM((2, N_SLOTS * 3 * n, DN_HEAD_DIM), BF16),
            pltpu.VMEM((2, N_SLOTS * 2 * n, DN_HEAD_DIM), F32),
            pltpu.VMEM((2, N_SLOTS * 8, LANES), F32),
            pltpu.VMEM((SEQ, DN_HEAD_DIM), F32),
        ],
        compiler_params=_params("arbitrary", "arbitrary"),
        name="deltanet",
    )(proj, proj, proj, proj, proj, qkv_conv, qkv_conv, qkv_conv, gates_tok, gates_t, proj, dn_norm, _tri_masks())


def _shift_down(y):
    row = lax.broadcasted_iota(jnp.int32, y.shape, 0)
    return jnp.where(row == 0, 0.0, pltpu.roll(y, 1, axis=0))


def _shift_up(y):
    row = lax.broadcasted_iota(jnp.int32, y.shape, 0)
    return jnp.where(row == y.shape[0] - 1, 0.0, pltpu.roll(y, y.shape[0] - 1, axis=0))


def _lru_gate_rows(xc, wcat, bcat, sp, a_ref, b_ref, rows):
    g = _mm(xc.astype(BF16), wcat) + bcat
    w = LRU_BLOCK_DIM
    for d in range(2):
        r = _sigmoid_tanh(g[:, (2 * d) * w:(2 * d + 1) * w])
        i = _sigmoid_tanh(g[:, (2 * d + 1) * w:(2 * d + 2) * w])
        a = jnp.exp(-LRU_C * r * sp[d:d + 1, :])
        a_ref[d, rows, :] = a
        b_ref[d, rows, :] = jnp.sqrt(1.0 - a * a) * (i * xc)


def _carry_chain(h_end, p_end, c0, reverse):
    row = lax.broadcasted_iota(jnp.int32, h_end.shape, 0)
    carries = jnp.zeros_like(h_end)
    c = c0
    order = range(LRU_SEGS - 1, -1, -1) if reverse else range(LRU_SEGS)
    for gseg in order:
        carries = jnp.where(row == gseg, c, carries)
        c = h_end[gseg:gseg + 1, :] + p_end[gseg:gseg + 1, :] * c
    return carries, c


def _lru_kernel(x_ref, xc_in_ref, gate_ref, cw_ref, cb_ref, wcat_ref, bcat_ref, lam_ref, o_ref,
                xc_ref, xcc_ref, a_ref, b_ref, ac_ref, bc_ref, hl_ref, pp_ref, hs_ref):
    cw = cw_ref[...]
    cb = cb_ref[...]
    wcat = wcat_ref[...]
    bcat = bcat_ref[...]
    sp = _softplus(-lam_ref[...])
    gw = GRID_W

    for i in range(CTX_LEN // DN_CHUNK):
        xcc_ref[pl.ds(i * DN_CHUNK, DN_CHUNK), :] = _conv4_rows(xc_in_ref, cw, i * DN_CHUNK, CTX_LEN) + cb
    _lru_gate_rows(xcc_ref[...], wcat, bcat, sp, ac_ref, bc_ref, pl.ds(0, CTX_LEN))

    csteps = CTX_LEN // LRU_SEGS
    zero = jnp.zeros((LRU_SEGS, LRU_BLOCK_DIM), F32)
    one = jnp.ones((LRU_SEGS, LRU_BLOCK_DIM), F32)

    def ctx_body(j, carry):
        hf, pf, hb, pb = carry
        rf = pl.ds(j, LRU_SEGS, stride=csteps)
        a = ac_ref[0, rf, :]
        hf = a * hf + bc_ref[0, rf, :]
        pf = a * pf
        rb = pl.ds(csteps - 1 - j, LRU_SEGS, stride=csteps)
        a = ac_ref[1, rb, :]
        hb = a * hb + bc_ref[1, rb, :]
        pb = a * pb
        return hf, pf, hb, pb

    hf, pf, hb, pb = lax.fori_loop(0, csteps, ctx_body, (zero, one, zero, one))
    zrow = jnp.zeros((1, LRU_BLOCK_DIM), F32)
    _, h0_f = _carry_chain(hf, pf, zrow, False)
    _, h0_b = _carry_chain(hb, pb, zrow, True)

    def slab(r):
        return x_ref[pl.ds(pl.multiple_of(r * gw, gw), gw), :]

    def conv_slab(r, xm2, xm1, x0, xp1):
        xc_ref[pl.ds(pl.multiple_of(r * gw, gw), gw), :] = (
            cb + cw[0:1] * xm2 + cw[1:2] * xm1 + cw[2:3] * x0 + cw[3:4] * xp1)

    last = GRID_ROWS - 1
    conv_slab(0, _shift_down(slab(last - 1)), _shift_down(slab(last)), slab(0), slab(1))
    conv_slab(1, _shift_down(slab(last)), slab(0), slab(1), slab(2))
    conv_slab(last, slab(last - 2), slab(last - 1), slab(last), _shift_up(slab(0)))

    def conv_body(r, _):
        conv_slab(r, slab(r - 2), slab(r - 1), slab(r), slab(r + 1))
        return 0

    lax.fori_loop(2, last, conv_body, 0)

    gate_rows = 512

    def gate_body(i, _):
        rows = pl.ds(pl.multiple_of(i * gate_rows, gate_rows), gate_rows)
        _lru_gate_rows(xc_ref[rows, :], wcat, bcat, sp, a_ref, b_ref, rows)
        return 0

    lax.fori_loop(0, SEQ // gate_rows, gate_body, 0)

    steps = SEQ // LRU_SEGS

    seg_stride = GRID_W // LRU_SEGS
    unroll = 8

    def seg_rows(wq, r):
        return pl.ds(r * GRID_W + wq, LRU_SEGS, stride=seg_stride)

    def scan_col(wq, carry):
        def scan_rows(rr, carry):
            hf, pf, hb, pb = carry
            for k in range(unroll):
                r = rr * unroll + k
                rf = seg_rows(wq, r)
                a = a_ref[0, rf, :]
                hf = a * hf + b_ref[0, rf, :]
                pf = a * pf
                hl_ref[0, wq * GRID_ROWS + r] = hf
                pp_ref[0, wq * GRID_ROWS + r] = pf
                wqb = seg_stride - 1 - wq
                rb = GRID_ROWS - 1 - r
                rbk = seg_rows(wqb, rb)
                a = a_ref[1, rbk, :]
                hb = a * hb + b_ref[1, rbk, :]
                pb = a * pb
                hl_ref[1, wqb * GRID_ROWS + rb] = hb
                pp_ref[1, wqb * GRID_ROWS + rb] = pb
            return hf, pf, hb, pb

        return lax.fori_loop(0, GRID_ROWS // unroll, scan_rows, carry)

    hf, pf, hb, pb = lax.fori_loop(0, seg_stride, scan_col, (zero, one, zero, one))
    c_f, _ = _carry_chain(hf, pf, h0_f, False)
    c_b, _ = _carry_chain(hb, pb, h0_b, True)

    def fix_col(wq, _):
        def fix_rows(rr, _):
            for k in range(unroll):
                r = rr * unroll + k
                j = wq * GRID_ROWS + r
                h = hl_ref[0, j] + pp_ref[0, j] * c_f + hl_ref[1, j] + pp_ref[1, j] * c_b
                hs_ref[seg_rows(wq, r), :] = h
            return 0

        return lax.fori_loop(0, GRID_ROWS // unroll, fix_rows, 0)

    lax.fori_loop(0, seg_stride, fix_col, 0)

    def out_body(i, _):
        rows = pl.ds(pl.multiple_of(i * gate_rows, gate_rows), gate_rows)
        o_ref[rows, :] = (hs_ref[rows, :] * _silu(gate_ref[rows, :])).astype(BF16)
        return 0

    lax.fori_loop(0, SEQ // gate_rows, out_body, 0)


def _lru(proj, conv_w, conv_b, wcat, bcat, lam):
    w = LRU_BLOCK_DIM
    steps = SEQ // LRU_SEGS
    ctx_blk0 = BATCH * SEQ // CTX_LEN
    return pl.pallas_call(
        _lru_kernel,
        out_shape=jax.ShapeDtypeStruct((BATCH * SEQ, LRU_WIDTH), BF16),
        grid=(BATCH, LRU_BLOCKS),
        in_specs=[
            pl.BlockSpec((SEQ, w), lambda b, n: (b, P_LRU // w + n)),
            pl.BlockSpec((CTX_LEN, w), lambda b, n: (ctx_blk0 + b, P_LRU // w + n)),
            pl.BlockSpec((SEQ, w), lambda b, n: (b, P_LRG // w + n)),
            pl.BlockSpec((LRU_CONV, w), lambda b, n: (0, n)),
            pl.BlockSpec((1, w), lambda b, n: (0, n)),
            pl.BlockSpec((None, w, 4 * w), lambda b, n: (n, 0, 0)),
            pl.BlockSpec((None, 1, 4 * w), lambda b, n: (n, 0, 0)),
            pl.BlockSpec((None, 2, w), lambda b, n: (n, 0, 0)),
        ],
        out_specs=pl.BlockSpec((SEQ, w), lambda b, n: (b, n)),
        scratch_shapes=[
            pltpu.VMEM((SEQ, w), F32),
            pltpu.VMEM((CTX_LEN, w), F32),
            pltpu.VMEM((2, SEQ, w), F32),
            pltpu.VMEM((2, SEQ, w), F32),
            pltpu.VMEM((2, CTX_LEN, w), F32),
            pltpu.VMEM((2, CTX_LEN, w), F32),
            pltpu.VMEM((2, steps, LRU_SEGS, w), F32),
            pltpu.VMEM((2, steps, LRU_SEGS, w), F32),
            pltpu.VMEM((SEQ, w), F32),
        ],
        compiler_params=_params("arbitrary", "arbitrary"),
        name="rglru",
    )(proj, proj, proj, conv_w, conv_b, wcat, bcat, lam)


def _out_proj_kernel(ya_ref, yb_ref, wa_ref, wb_ref, x_ref, g_ref, o_ref, wcast_ref):
    @pl.when(pl.program_id(1) == 0)
    def _():
        def cast_body(r, _):
            rows = pl.ds(pl.multiple_of(r * W_CAST_ROWS, W_CAST_ROWS), W_CAST_ROWS)
            wcast_ref[0, rows, :] = wa_ref[rows, :].astype(BF16)
            wcast_ref[1, rows, :] = wb_ref[rows, :].astype(BF16)
            return 0

        lax.fori_loop(0, wa_ref.shape[0] // W_CAST_ROWS, cast_body, 0)

    acc = _mm(ya_ref[...], wcast_ref[0]) + _mm(yb_ref[...], wcast_ref[1])
    o_ref[...] = x_ref[...] + g_ref[...] * acc


def _out_proj(ya, yb, ya_blk, yb_blk, w, x2d, gate3, tm=1024, tn=512):
    m, dm = x2d.shape
    kh = w.shape[0] // 2
    per_batch = SEQ // tm
    return pl.pallas_call(
        _out_proj_kernel,
        out_shape=jax.ShapeDtypeStruct((m, dm), F32),
        grid=(dm // tn, m // tm),
        in_specs=[
            pl.BlockSpec((tm, kh), lambda j, i: (i, ya_blk)),
            pl.BlockSpec((tm, kh), lambda j, i: (i, yb_blk)),
            pl.BlockSpec((kh, tn), lambda j, i: (0, j)),
            pl.BlockSpec((kh, tn), lambda j, i: (1, j)),
            pl.BlockSpec((tm, tn), lambda j, i: (i, j)),
            pl.BlockSpec((None, 1, tn), lambda j, i: (i // per_batch, 0, j)),
        ],
        out_specs=pl.BlockSpec((tm, tn), lambda j, i: (i, j)),
        scratch_shapes=[pltpu.VMEM((2, kh, tn), BF16)],
        compiler_params=_params("arbitrary", "arbitrary"),
        name="out_proj",
    )(ya, yb, w, w, x2d, gate3)


def _sc_proj_kernel(a_ref, wb_ref, wc_ref, wx_ref, wg_ref, cw_ref, o_ref, wcast_ref, z_ref, bg_ref):
    @pl.when(pl.program_id(1) == 0)
    def _():
        def cast_body(r, _):
            rows = pl.ds(pl.multiple_of(r * W_CAST_ROWS, W_CAST_ROWS), W_CAST_ROWS)
            for gi, w_ref in enumerate((wb_ref, wc_ref, wx_ref, wg_ref)):
                wcast_ref[gi, rows, :] = w_ref[rows, :].astype(BF16)
            return 0

        lax.fori_loop(0, wb_ref.shape[0] // W_CAST_ROWS, cast_body, 0)

    hn = a_ref[...]
    z_ref[...] = _mm(hn, wcast_ref[1]) * _mm(hn, wcast_ref[2])
    bg_ref[...] = _mm(hn, wcast_ref[0]) * _silu(_mm(hn, wcast_ref[3]))
    cw = cw_ref[...]
    gw = GRID_W

    def body(r, _):
        rows = pl.ds(pl.multiple_of(r * gw, gw), gw)
        z = z_ref[rows, :]
        zc = cw[0:1] * _shift_down(z) + cw[1:2] * z + cw[2:3] * _shift_up(z)
        o_ref[rows, :] = (bg_ref[rows, :] * zc).astype(BF16)
        return 0

    lax.fori_loop(0, a_ref.shape[0] // gw, body, 0)


def _sc_proj(hn, w, conv_w, tm=512, tc=256):
    m, dm = hn.shape
    nt = SC_WIDTH // tc
    return pl.pallas_call(
        _sc_proj_kernel,
        out_shape=jax.ShapeDtypeStruct((m, SC_WIDTH), BF16),
        grid=(nt, m // tm),
        in_specs=[
            pl.BlockSpec((tm, dm), lambda j, i: (i, 0)),
            pl.BlockSpec((dm, tc), lambda j, i: (0, j)),
            pl.BlockSpec((dm, tc), lambda j, i: (0, nt + j)),
            pl.BlockSpec((dm, tc), lambda j, i: (0, 2 * nt + j)),
            pl.BlockSpec((dm, tc), lambda j, i: (0, 3 * nt + j)),
            pl.BlockSpec((SC_CONV, tc), lambda j, i: (0, j)),
        ],
        out_specs=pl.BlockSpec((tm, tc), lambda j, i: (i, j)),
        scratch_shapes=[pltpu.VMEM((4, dm, tc), BF16), pltpu.VMEM((tm, tc), F32), pltpu.VMEM((tm, tc), F32)],
        compiler_params=_params("arbitrary", "arbitrary"),
        name="sc_proj",
    )(hn, w, w, w, w, conv_w)


def _final_norm_kernel(x_ref, w_ref, o_ref):
    w = w_ref[...]

    def body(r, _):
        rows = pl.ds(pl.multiple_of(r * NORM_ROWS, NORM_ROWS), NORM_ROWS)
        x = x_ref[rows, :]
        ms = jnp.mean(x * x, axis=-1, keepdims=True)
        o_ref[rows, :] = x * lax.rsqrt(ms + EPS) * w
        return 0

    lax.fori_loop(0, x_ref.shape[0] // NORM_ROWS, body, 0, unroll=4)


def _final_norm(x2d, w, tm=256):
    m, dm = x2d.shape
    return pl.pallas_call(
        _final_norm_kernel,
        out_shape=jax.ShapeDtypeStruct((m, dm), F32),
        grid=(m // tm,),
        in_specs=[pl.BlockSpec((tm, dm), lambda i: (i, 0)), pl.BlockSpec((1, dm), lambda i: (0, 0))],
        out_specs=pl.BlockSpec((tm, dm), lambda i: (i, 0)),
        compiler_params=_params("arbitrary"),
        name="final_norm",
    )(x2d, w)


def _gate_lane_vector(p):
    z = jnp.zeros((DN_HEADS,), F32)
    cols = jnp.stack([z, z, p[0], p[1], p[0], p[1], z, z], axis=1)
    return cols.reshape(1, LANES).astype(F32)


def _gate_row_selection():
    p = np.zeros((LANES, LANES), np.float32)
    for h in range(DN_HEADS):
        for slot, src in enumerate([h, DN_HEADS + h, 2 * DN_HEADS + h, 3 * DN_HEADS + h, 2 * DN_HEADS + h, 3 * DN_HEADS + h]):
            p[h * GATE_SLOTS + slot, src] = 1.0
    return jnp.asarray(p, BF16)


def kernel(x, c, ctx, c_ctx, mod_w, mod_b, norm_w, ab_w_in, ab_qkv_conv, ab_a_log, ab_dt_bias, ab_dn_norm,
           ab_lru_conv_w, ab_lru_conv_b, ab_lru_w_r, ab_lru_b_r, ab_lru_w_i, ab_lru_b_i, ab_lru_lambda, ab_w_out,
           sc_w_in, sc_conv, sc_w_out, final_norm_w):
    dm = D_MODEL
    x2d = x.reshape(BATCH * SEQ, dm)
    ctx2d = ctx.reshape(BATCH * CTX_LEN, dm)

    c8 = jnp.concatenate([c, c_ctx[None, :], jnp.zeros((8 - BATCH - 1, dm), F32)], axis=0)
    mod = _modulation(c8, mod_w, mod_b)
    shift = [mod[l, :, :dm].reshape(8, 1, dm) for l in range(2)]
    scale = [mod[l, :, dm:2 * dm].reshape(8, 1, dm) for l in range(2)]
    gate = [mod[l, :, 2 * dm:].reshape(8, 1, dm) for l in range(2)]

    w_in_t = jnp.swapaxes(ab_w_in[0], 0, 1)
    nw0 = norm_w[0].reshape(1, dm)
    hn = _norm_mod(x2d, ctx2d, nw0, scale[0], shift[0])
    proj = _proj0(hn, w_in_t)
    ba = _gate_proj(hn, w_in_t)

    n_lat = BATCH * SEQ
    ba_all = jnp.concatenate([ba[n_lat:].reshape(BATCH, CTX_LEN, LANES), ba[:n_lat].reshape(BATCH, SEQ, LANES)], axis=1)
    gates_tok = _gates(ba_all.reshape(BATCH * ALL_LEN, LANES), _gate_lane_vector(ab_a_log[0]),
                       _gate_lane_vector(ab_dt_bias[0])).reshape(BATCH, ALL_LEN, LANES)
    gates_t = gates_tok.transpose(0, 2, 1)

    y_dn = _deltanet(proj, ab_qkv_conv[0], gates_tok, gates_t, ab_dn_norm[0].reshape(1, DN_HEAD_DIM))

    wcat = jnp.concatenate([ab_lru_w_r[0, 0], ab_lru_w_i[0, 0], ab_lru_w_r[0, 1], ab_lru_w_i[0, 1]],
                           axis=-1).astype(BF16)
    bl = LRU_BLOCK_DIM
    bcat = jnp.concatenate([ab_lru_b_r[0, 0].reshape(LRU_BLOCKS, 1, bl), ab_lru_b_i[0, 0].reshape(LRU_BLOCKS, 1, bl),
                            ab_lru_b_r[0, 1].reshape(LRU_BLOCKS, 1, bl), ab_lru_b_i[0, 1].reshape(LRU_BLOCKS, 1, bl)],
                           axis=-1)
    lam = ab_lru_lambda[0].reshape(2, LRU_BLOCKS, bl).transpose(1, 0, 2)
    y_lru = _lru(proj, ab_lru_conv_w[0], ab_lru_conv_b[0].reshape(1, LRU_WIDTH), wcat, bcat, lam)

    x1 = _out_proj(y_dn, y_lru, 0, 0, ab_w_out[0], x2d, gate[0])

    hn1 = _norm_mod(x1, None, norm_w[1].reshape(1, dm), scale[1], shift[1])
    y1 = _sc_proj(hn1, sc_w_in[0], sc_conv[0])
    x2 = _out_proj(y1, y1, 0, 1, sc_w_out[0], x1, gate[1])

    return _final_norm(x2, final_norm_w.reshape(1, dm)).reshape(BATCH, SEQ, dm)
```
